```python
import math
import jax, jax.numpy as jnp
from jax import lax
import numpy as np

D_MODEL = 2048
BATCH = 4
SEQ = 4096
DEPTH = 4

N_MIXERS = 2
N_RET_LAYERS = (DEPTH + 1) // 2
N_POOL_LAYERS = DEPTH // 2
EPS = 1e-6

N_RET_HEADS = 8
RET_QK_DIM = D_MODEL
RET_V_DIM = 2 * D_MODEL
RET_HEAD_QK = RET_QK_DIM // N_RET_HEADS
RET_HEAD_V = RET_V_DIM // N_RET_HEADS
RET_GATE_DIM = RET_V_DIM
RET_IN_COLS = 2 * RET_QK_DIM + RET_V_DIM + RET_GATE_DIM
RET_CHUNK = 128
ROPE_BASE = 10000.0

POOL_WIDTH = 2 * D_MODEL
POOL_WINDOWS = (2, 4, 8, 16)
POOL_GROUPS = len(POOL_WINDOWS)
POOL_GROUP_DIM = POOL_WIDTH // POOL_GROUPS

kernel_name = "hybrid_retention_pooling_gated"


def rms_norm(x, g):
    xf = x.astype(jnp.float32)
    y = xf * lax.rsqrt(jnp.mean(xf * xf, axis=-1, keepdims=True) + EPS)
    return (y * g.astype(jnp.float32)).astype(x.dtype)


def rope(x, seq_len):
    d = x.shape[-1]
    half = d // 2
    inv = ROPE_BASE ** (-jnp.arange(half, dtype=jnp.float32) / half)
    ang = jnp.arange(seq_len, dtype=jnp.float32)[:, None] * inv[None, :]
    cos = jnp.cos(ang)[None, :, None, :]
    sin = jnp.sin(ang)[None, :, None, :]
    x1, x2 = x[..., :half], x[..., half:]
    return jnp.concatenate([x1 * cos - x2 * sin, x1 * sin + x2 * cos], axis=-1)


def retention_log_decay():
    h = jnp.arange(N_RET_HEADS, dtype=jnp.float32)
    return jnp.log1p(-jnp.exp2(-5.0 - h))


def chunkwise_retention(q, k, v):
    B, S, H, dk = q.shape
    dv = v.shape[-1]
    C = RET_CHUNK
    NC = S // C
    lg = retention_log_decay()
    idx = jnp.arange(C, dtype=jnp.float32)
    diff = idx[:, None] - idx[None, :]
    mask = jnp.where(diff[None] >= 0,
                     jnp.exp(jnp.maximum(diff, 0.0)[None] * lg[:, None, None]), 0.0)
    q_decay = jnp.exp((idx[None, :] + 1.0) * lg[:, None])
    k_decay = jnp.exp((C - 1.0 - idx[None, :]) * lg[:, None])
    chunk_decay = jnp.exp(C * lg)

    def to_chunks(t):
        return t.reshape(B, NC, C, H, t.shape[-1]).transpose(1, 0, 3, 2, 4)

    qc, kc, vc = to_chunks(q), to_chunks(k), to_chunks(v)

    def body(state, inp):
        qi, ki, vi = inp
        scores = jnp.einsum('bhid,bhjd->bhij', qi, ki) * mask[None]
        inner = jnp.einsum('bhij,bhje->bhie', scores, vi)
        cross = jnp.einsum('bhid,bhde->bhie', qi * q_decay[None, :, :, None], state)
        new_state = (chunk_decay[None, :, None, None] * state
                     + jnp.einsum('bhjd,bhje->bhde', ki * k_decay[None, :, :, None], vi))
        return new_state, inner + cross

    state0 = jnp.zeros((B, H, dk, dv), jnp.float32)
    _, out = lax.scan(body, state0, (qc, kc, vc))
    return out.transpose(1, 0, 3, 2, 4).reshape(B, S, H, dv)


def retention_layer(x, g_norm, w_in, gn, w_out):
    B, S, _ = x.shape
    h = rms_norm(x, g_norm)
    p = h @ w_in
    q, k, v, z = jnp.split(p, [RET_QK_DIM, 2 * RET_QK_DIM, 2 * RET_QK_DIM + RET_V_DIM], axis=-1)
    q = rope(q.astype(jnp.float32).reshape(B, S, N_RET_HEADS, RET_HEAD_QK), S)
    k = rope(k.astype(jnp.float32).reshape(B, S, N_RET_HEADS, RET_HEAD_QK), S) * (RET_HEAD_QK ** -0.5)
    v = v.astype(jnp.float32).reshape(B, S, N_RET_HEADS, RET_HEAD_V)
    o = chunkwise_retention(q, k, v)
    mu = jnp.mean(o, axis=-1, keepdims=True)
    var = jnp.mean(jnp.square(o - mu), axis=-1, keepdims=True)
    o = ((o - mu) * lax.rsqrt(var + EPS)).reshape(B, S, RET_V_DIM) * gn.astype(jnp.float32)
    o = o.astype(x.dtype) * jax.nn.silu(z)
    return o @ w_out


def pooling_layer(x, g_norm, w_in, w_grp, b_grp, scale, w_out):
    B, S, _ = x.shape
    h = rms_norm(x, g_norm)
    p = h @ w_in
    u, z = jnp.split(p, [POOL_WIDTH], axis=-1)
    u = u.reshape(B, S, POOL_GROUPS, POOL_GROUP_DIM)
    uf = u.astype(jnp.float32)
    cs = jnp.concatenate([jnp.zeros((B, 1, POOL_GROUPS, POOL_GROUP_DIM), jnp.float32),
                          jnp.cumsum(uf, axis=1)], axis=1)
    t = jnp.arange(S)
    pooled = []
    for gi, w in enumerate(POOL_WINDOWS):
        lo = jnp.maximum(t + 1 - w, 0)
        cnt = jnp.minimum(t + 1, w).astype(jnp.float32)[None, :, None]
        win_sum = cs[:, 1:, gi] - jnp.take(cs[:, :, gi], lo, axis=1)
        pooled.append(win_sum / cnt)
    pooled = jnp.stack(pooled, axis=2)
    mix = (pooled - uf).astype(x.dtype)
    mix = jnp.einsum('bsgc,gcd->bsgd', mix, w_grp) + b_grp
    mix = mix.reshape(B, S, POOL_WIDTH) * scale
    return (mix * jax.nn.silu(z)) @ w_out


def setup_inputs(seed: int = 0) -> dict:
    key = jax.random.key(seed)
    ks = jax.random.split(key, 13)
    f32 = jnp.float32
    nr, npl = N_RET_LAYERS, N_POOL_LAYERS
    return {
        "x": jax.random.normal(ks[0], (BATCH, SEQ, D_MODEL), f32),
        "ret_norm": 1.0 + 0.05 * jax.random.normal(ks[1], (nr, D_MODEL), f32),
        "ret_w_in": jax.random.normal(ks[2], (nr, D_MODEL, RET_IN_COLS), f32) * D_MODEL ** -0.5,
        "ret_gn": 1.0 + 0.05 * jax.random.normal(ks[3], (nr, RET_V_DIM), f32),
        "ret_w_out": jax.random.normal(ks[4], (nr, RET_V_DIM, D_MODEL), f32) * RET_V_DIM ** -0.5,
        "pool_norm": 1.0 + 0.05 * jax.random.normal(ks[5], (npl, D_MODEL), f32),
        "pool_w_in": jax.random.normal(ks[6], (npl, D_MODEL, 2 * POOL_WIDTH), f32) * D_MODEL ** -0.5,
        "pool_w_grp": jax.random.normal(ks[7], (npl, POOL_GROUPS, POOL_GROUP_DIM, POOL_GROUP_DIM), f32) * POOL_GROUP_DIM ** -0.5,
        "pool_b_grp": 0.01 * jax.random.normal(ks[8], (npl, POOL_GROUPS, POOL_GROUP_DIM), f32),
        "pool_scale": 1.0 + 0.1 * jax.random.normal(ks[9], (npl, POOL_WIDTH), f32),
        "pool_w_out": jax.random.normal(ks[10], (npl, POOL_WIDTH, D_MODEL), f32) * POOL_WIDTH ** -0.5,
        "final_norm": 1.0 + 0.05 * jax.random.normal(ks[11], (D_MODEL,), f32),
    }


def reference(x, ret_norm, ret_w_in, ret_gn, ret_w_out, pool_norm, pool_w_in,
              pool_w_grp, pool_b_grp, pool_scale, pool_w_out, final_norm):
    h = x
    for i in range(DEPTH):
        j = i // N_MIXERS
        if i % N_MIXERS == 0:
            h = h + retention_layer(h, ret_norm[j], ret_w_in[j], ret_gn[j], ret_w_out[j])
        else:
            h = h + pooling_layer(h, pool_norm[j], pool_w_in[j], pool_w_grp[j],
                                  pool_b_grp[j], pool_scale[j], pool_w_out[j])
    return rms_norm(h, final_norm)
```

```python
import functools

import jax
import jax.numpy as jnp
from jax import lax
from jax.experimental import pallas as pl
from jax.experimental.pallas import tpu as pltpu

F32 = jnp.float32
BF16 = jnp.bfloat16

EPS = 1e-6
N_RET_HEADS = 8
HEAD_QK = 256
HEAD_V = 512
ROPE_HALF = HEAD_QK // 2
ROPE_BASE = 10000.0
POOL_WINDOWS = (2, 4, 8, 16)
POOL_HALO = 16
RET_CHUNK = 256

VMEM_LIMIT = 56 * 1024 * 1024


def _params(sem):
    return pltpu.CompilerParams(dimension_semantics=sem, vmem_limit_bytes=VMEM_LIMIT)


def _rmsnorm_kernel(x_ref, g_ref, o_ref):
    x = x_ref[...]
    ms = jnp.mean(x * x, axis=-1, keepdims=True)
    o_ref[...] = (x * lax.rsqrt(ms + EPS) * g_ref[...]).astype(o_ref.dtype)


def _rmsnorm(x, g, out_dtype, tr=512):
    t, d = x.shape
    return pl.pallas_call(
        _rmsnorm_kernel,
        out_shape=jax.ShapeDtypeStruct((t, d), out_dtype),
        grid=(t // tr,),
        in_specs=[pl.BlockSpec((tr, d), lambda i: (i, 0)),
                  pl.BlockSpec((1, d), lambda i: (0, 0))],
        out_specs=pl.BlockSpec((tr, d), lambda i: (i, 0)),
        compiler_params=_params(("arbitrary",)),
        name="rmsnorm",
    )(x, g.reshape(1, d))


def _proj_in_kernel(a_ref, w_ref, cos_ref, sin_ref, o_ref, *, n_rope_tiles, n_q_tiles, tn):
    j = pl.program_id(1)
    acc = jnp.dot(a_ref[...], w_ref[...], preferred_element_type=F32)

    if n_rope_tiles == 0:
        o_ref[...] = acc.astype(o_ref.dtype)
        return

    @pl.when(j < n_rope_tiles)
    def _():
        cos = cos_ref[...]
        sin = sin_ref[...]
        sc = jnp.where(j < n_q_tiles, 1.0, HEAD_QK ** -0.5).astype(F32)
        for hh in range(tn // HEAD_QK):
            a = hh * HEAD_QK
            x1 = acc[:, a:a + ROPE_HALF]
            x2 = acc[:, a + ROPE_HALF:a + HEAD_QK]
            o_ref[:, a:a + ROPE_HALF] = ((x1 * cos - x2 * sin) * sc).astype(o_ref.dtype)
            o_ref[:, a + ROPE_HALF:a + HEAD_QK] = ((x1 * sin + x2 * cos) * sc).astype(o_ref.dtype)

    @pl.when(j >= n_rope_tiles)
    def _():
        o_ref[...] = acc.astype(o_ref.dtype)


def _proj_in(a, w, cos, sin, *, rope_cols, q_cols, seq, tm=1024, tn=1024):
    t, kdim = a.shape
    n = w.shape[1]
    bps = seq // tm
    kern = functools.partial(_proj_in_kernel, n_rope_tiles=rope_cols // tn,
                             n_q_tiles=q_cols // tn, tn=tn)
    return pl.pallas_call(
        kern,
        out_shape=jax.ShapeDtypeStruct((t, n), BF16),
        grid=(t // tm, n // tn),
        in_specs=[pl.BlockSpec((tm, kdim), lambda i, j: (i, 0)),
                  pl.BlockSpec((kdim, tn), lambda i, j: (0, j)),
                  pl.BlockSpec((tm, ROPE_HALF), lambda i, j: (i % bps, 0)),
                  pl.BlockSpec((tm, ROPE_HALF), lambda i, j: (i % bps, 0))],
        out_specs=pl.BlockSpec((tm, tn), lambda i, j: (i, j)),
        compiler_params=_params(("arbitrary", "arbitrary")),
        name="proj_in",
    )(a, w, cos, sin)


def _proj_out_kernel(a_ref, w_ref, r_ref, o_ref):
    o_ref[...] = r_ref[...] + jnp.dot(a_ref[...], w_ref[...], preferred_element_type=F32)


def _proj_out(a, w, resid, tm=1024, tn=512):
    t, kdim = a.shape
    n = w.shape[1]
    return pl.pallas_call(
        _proj_out_kernel,
        out_shape=jax.ShapeDtypeStruct((t, n), F32),
        grid=(t // tm, n // tn),
        in_specs=[pl.BlockSpec((tm, kdim), lambda i, j: (i, 0)),
                  pl.BlockSpec((kdim, tn), lambda i, j: (0, j)),
                  pl.BlockSpec((tm, tn), lambda i, j: (i, j))],
        out_specs=pl.BlockSpec((tm, tn), lambda i, j: (i, j)),
        compiler_params=_params(("arbitrary", "arbitrary")),
        name="proj_out",
    )(a, w, resid)


def _silu(z):
    return z * (1.0 / (1.0 + jnp.exp(-z)))


def _retention_kernel(lg_ref, q_ref, k_ref, v_ref, z_ref, gn_ref, o_ref, state_ref, *, chunk, n_sub):
    h = pl.program_id(1)
    c = pl.program_id(2)

    @pl.when(c == 0)
    def _():
        state_ref[...] = jnp.zeros_like(state_ref)

    lg = lg_ref[h]
    ri = lax.broadcasted_iota(jnp.int32, (chunk, 1), 0).astype(F32)
    q_decay = jnp.exp((ri + 1.0) * lg)
    k_decay = jnp.exp((chunk - 1.0 - ri) * lg)
    chunk_decay = jnp.exp(jnp.full((1, 1), float(chunk), F32) * lg)
    ii = lax.broadcasted_iota(jnp.int32, (chunk, chunk), 0)
    jj = lax.broadcasted_iota(jnp.int32, (chunk, chunk), 1)
    diff = (ii - jj).astype(F32)
    mask = jnp.where(diff >= 0, jnp.exp(jnp.maximum(diff, 0.0) * lg), 0.0)
    gn = gn_ref[...]

    for s in range(n_sub):
        rows = pl.ds(s * chunk, chunk)
        q = q_ref[rows, :]
        k = k_ref[rows, :]
        v = v_ref[rows, :]
        scores = lax.dot_general(q, k, (((1,), (1,)), ((), ())), preferred_element_type=F32)
        inner = jnp.dot((scores * mask).astype(BF16), v, preferred_element_type=F32)
        state = state_ref[...]
        qd = (q.astype(F32) * q_decay).astype(BF16)
        cross = jnp.dot(qd, state.astype(BF16), preferred_element_type=F32)
        kd = (k.astype(F32) * k_decay).astype(BF16)
        upd = lax.dot_general(kd, v, (((0,), (0,)), ((), ())), preferred_element_type=F32)
        state_ref[...] = chunk_decay * state + upd

        o = inner + cross
        mu = jnp.mean(o, axis=-1, keepdims=True)
        oc = o - mu
        var = jnp.mean(oc * oc, axis=-1, keepdims=True)
        y = oc * lax.rsqrt(var + EPS) * gn
        z = z_ref[rows, :].astype(F32)
        o_ref[rows, :] = (y * _silu(z)).astype(o_ref.dtype)


def _retention(p, lg, gn, *, batch, seq, rows=1024):
    t = p.shape[0]
    nh = N_RET_HEADS
    cps = seq // rows
    kern = functools.partial(_retention_kernel, chunk=RET_CHUNK, n_sub=rows // RET_CHUNK)
    k_off = nh
    v_off = (2 * nh * HEAD_QK) // HEAD_V
    z_off = v_off + nh
    return pl.pallas_call(
        kern,
        out_shape=jax.ShapeDtypeStruct((t, nh * HEAD_V), BF16),
        grid=(batch, nh, cps),
        in_specs=[pl.BlockSpec(memory_space=pltpu.SMEM),
                  pl.BlockSpec((rows, HEAD_QK), lambda b, h, c: (b * cps + c, h)),
                  pl.BlockSpec((rows, HEAD_QK), lambda b, h, c: (b * cps + c, k_off + h)),
                  pl.BlockSpec((rows, HEAD_V), lambda b, h, c: (b * cps + c, v_off + h)),
                  pl.BlockSpec((rows, HEAD_V), lambda b, h, c: (b * cps + c, z_off + h)),
                  pl.BlockSpec((1, HEAD_V), lambda b, h, c: (0, h))],
        out_specs=pl.BlockSpec((rows, HEAD_V), lambda b, h, c: (b * cps + c, h)),
        scratch_shapes=[pltpu.VMEM((HEAD_QK, HEAD_V), F32)],
        compiler_params=_params(("arbitrary", "arbitrary", "arbitrary")),
        name="retention",
    )(lg, p, p, p, p, gn.reshape(1, nh * HEAD_V))


def _pool_mix_kernel(u_ref, halo_ref, z_ref, w_ref, b_ref, sc_ref, o_ref, *, tm, bps, gdim):
    i = pl.program_id(0)
    blk = i % bps
    keep_halo = (blk != 0).astype(F32)
    tpos = (blk * tm + lax.broadcasted_iota(jnp.int32, (tm, 1), 0)).astype(F32)
    for g, win in enumerate(POOL_WINDOWS):
        cols = slice(g * gdim, (g + 1) * gdim)
        u = u_ref[:, cols].astype(F32)
        halo = halo_ref[:, cols].astype(F32) * keep_halo
        s = jnp.concatenate([halo, u], axis=0)
        shift = 1
        while shift < win:
            s = s + pltpu.roll(s, shift, 0)
            shift *= 2
        win_sum = s[POOL_HALO:, :]
        inv_cnt = 1.0 / jnp.minimum(tpos + 1.0, float(win))
        mix = (win_sum * inv_cnt - u).astype(BF16)
        y = jnp.dot(mix, w_ref[g], preferred_element_type=F32) + b_ref[:, cols]
        z = z_ref[:, cols].astype(F32)
        o_ref[:, cols] = (y * sc_ref[:, cols] * _silu(z)).astype(o_ref.dtype)


def _pool_mix(p, w_grp, b_grp, scale, *, seq, tm=512):
    t = p.shape[0]
    ng, gdim, _ = w_grp.shape
    width = ng * gdim
    bps = seq // tm
    hb = tm // POOL_HALO
    kern = functools.partial(_pool_mix_kernel, tm=tm, bps=bps, gdim=gdim)
    return pl.pallas_call(
        kern,
        out_shape=jax.ShapeDtypeStruct((t, width), BF16),
        grid=(t // tm,),
        in_specs=[pl.BlockSpec((tm, width), lambda i: (i, 0)),
                  pl.BlockSpec((POOL_HALO, width), lambda i: (jnp.maximum(i * hb - 1, 0), 0)),
                  pl.BlockSpec((tm, width), lambda i: (i, 1)),
                  pl.BlockSpec((ng, gdim, gdim), lambda i: (0, 0, 0)),
                  pl.BlockSpec((1, width), lambda i: (0, 0)),
                  pl.BlockSpec((1, width), lambda i: (0, 0))],
        out_specs=pl.BlockSpec((tm, width), lambda i: (i, 0)),
        compiler_params=_params(("arbitrary",)),
        name="pool_mix",
    )(p, p, p, w_grp, b_grp.reshape(1, width), scale.reshape(1, width))


def _rope_tables(seq):
    inv = ROPE_BASE ** (-jnp.arange(ROPE_HALF, dtype=F32) / ROPE_HALF)
    ang = jnp.arange(seq, dtype=F32)[:, None] * inv[None, :]
    return jnp.cos(ang), jnp.sin(ang)


def kernel(x, ret_norm, ret_w_in, ret_gn, ret_w_out, pool_norm, pool_w_in, pool_w_grp, pool_b_grp, pool_scale, pool_w_out, final_norm):
    batch, seq, d = x.shape
    depth = ret_norm.shape[0] + pool_norm.shape[0]
    qk_cols = N_RET_HEADS * HEAD_QK
    cos, sin = _rope_tables(seq)
    lg = jnp.log1p(-jnp.exp2(-5.0 - jnp.arange(N_RET_HEADS, dtype=F32)))

    h = x.reshape(batch * seq, d)
    for i in range(depth):
        j = i // 2
        if i % 2 == 0:
            hn = _rmsnorm(h, ret_norm[j], BF16)
            p = _proj_in(hn, ret_w_in[j].astype(BF16), cos, sin,
                         rope_cols=2 * qk_cols, q_cols=qk_cols, seq=seq)
            og = _retention(p, lg, ret_gn[j], batch=batch, seq=seq)
            h = _proj_out(og, ret_w_out[j].astype(BF16), h)
        else:
            hn = _rmsnorm(h, pool_norm[j], BF16)
            p = _proj_in(hn, pool_w_in[j].astype(BF16), cos, sin,
                         rope_cols=0, q_cols=0, seq=seq)
            mg = _pool_mix(p, pool_w_grp[j].astype(BF16), pool_b_grp[j], pool_scale[j], seq=seq)
            h = _proj_out(mg, pool_w_out[j].astype(BF16), h)
    out = _rmsnorm(h, final_norm, x.dtype)
    return out.reshape(batch, seq, d)
```

```python
import functools

import jax
import jax.numpy as jnp
from jax import lax
from jax.experimental import pallas as pl
from jax.experimental.pallas import tpu as pltpu

F32 = jnp.float32
BF16 = jnp.bfloat16

EPS = 1e-6
N_RET_HEADS = 8
HEAD_QK = 256
HEAD_V = 512
ROPE_HALF = HEAD_QK // 2
ROPE_BASE = 10000.0
POOL_WINDOWS = (2, 4, 8, 16)
POOL_LEVELS = 4
POOL_HALO = 32
RET_CHUNK = 256

VMEM_LIMIT = 56 * 1024 * 1024


def _params(sem):
    return pltpu.CompilerParams(dimension_semantics=sem, vmem_limit_bytes=VMEM_LIMIT)


def _silu(z):
    return z * (1.0 / (1.0 + jnp.exp(-z)))


def _rmsnorm_kernel(x_ref, g_ref, o_ref):
    x = x_ref[...]
    ms = jnp.mean(x * x, axis=-1, keepdims=True)
    o_ref[...] = (x * lax.rsqrt(ms + EPS) * g_ref[...]).astype(o_ref.dtype)


def _rmsnorm(x, g, layer, out_dtype, tr=512):
    t, d = x.shape
    return pl.pallas_call(
        _rmsnorm_kernel,
        out_shape=jax.ShapeDtypeStruct((t, d), out_dtype),
        grid=(t // tr,),
        in_specs=[pl.BlockSpec((tr, d), lambda i: (i, 0)),
                  pl.BlockSpec((None, 1, d), lambda i: (layer, 0, 0))],
        out_specs=pl.BlockSpec((tr, d), lambda i: (i, 0)),
        compiler_params=_params(("arbitrary",)),
        name="rmsnorm",
    )(x, g.reshape(g.shape[0], 1, d))


def _proj_plain_kernel(a_ref, w_ref, o_ref):
    acc = jnp.dot(a_ref[...], w_ref[...], preferred_element_type=F32)
    o_ref[...] = acc.astype(o_ref.dtype)


def _proj_silu_kernel(a_ref, w_ref, o_ref):
    acc = jnp.dot(a_ref[...], w_ref[...], preferred_element_type=F32)
    o_ref[...] = _silu(acc).astype(o_ref.dtype)


def _proj_gate_kernel(a_ref, w_ref, y_ref, b_ref, sc_ref, o_ref):
    z = jnp.dot(a_ref[...], w_ref[...], preferred_element_type=F32)
    mix = (y_ref[...] + b_ref[...]) * sc_ref[...]
    o_ref[...] = (mix * _silu(z)).astype(o_ref.dtype)


def _proj_rope_kernel(lg_ref, a_ref, w_ref, cos_ref, sin_ref, o_ref, *, tm, tn, chunk):
    j = pl.program_id(0)
    acc = jnp.dot(a_ref[...], w_ref[...], preferred_element_type=F32)
    cos = cos_ref[...]
    sin = sin_ref[...]
    row = lax.broadcasted_iota(jnp.int32, (tm, 1), 0)
    pos1 = ((row % chunk) + 1).astype(F32)
    heads_per_tile = tn // HEAD_QK
    for hh in range(heads_per_tile):
        slot = j * heads_per_tile + hh
        fac = jnp.exp(pos1 * lg_ref[slot])
        fac = fac * jnp.where(slot < N_RET_HEADS, 1.0, HEAD_QK ** -0.5).astype(F32)
        cf = cos * fac
        sf = sin * fac
        a = hh * HEAD_QK
        x1 = acc[:, a:a + ROPE_HALF]
        x2 = acc[:, a + ROPE_HALF:a + HEAD_QK]
        o_ref[:, a:a + ROPE_HALF] = (x1 * cf - x2 * sf).astype(o_ref.dtype)
        o_ref[:, a + ROPE_HALF:a + HEAD_QK] = (x1 * sf + x2 * cf).astype(o_ref.dtype)


def _proj_pool_kernel(a_ref, w_ref, wg_ref, y_ref, ext_ref, tail_ref, *, tm, bps):
    g = pl.program_id(0)
    i = pl.program_id(1)
    blk = i % bps
    n = POOL_HALO + tm

    @pl.when(i == 0)
    def _():
        tail_ref[...] = jnp.zeros_like(tail_ref)

    u = jnp.dot(a_ref[...], w_ref[...], preferred_element_type=F32)
    halo = jnp.where(blk == 0, 0.0, tail_ref[...])
    ext_ref[0, 0:POOL_HALO, :] = halo
    ext_ref[0, POOL_HALO:n, :] = u
    for lvl in range(POOL_LEVELS):
        src, dst = lvl % 2, (lvl + 1) % 2
        lo = 8 * (lvl + 1)
        on = (lvl <= g).astype(F32)
        ext_ref[dst, lo:n, :] = (ext_ref[src, lo:n, :]
                                 + on * ext_ref[src, pl.ds(lo - (1 << lvl), n - lo), :])
    win_sum = ext_ref[POOL_LEVELS % 2, POOL_HALO:n, :]
    tpos = (blk * tm + lax.broadcasted_iota(jnp.int32, (tm, 1), 0)).astype(F32)
    win = lax.shift_left(jnp.int32(2), g).astype(F32)
    inv_cnt = 1.0 / jnp.minimum(tpos + 1.0, win)
    mix = (win_sum * inv_cnt - u).astype(BF16)
    y_ref[...] = jnp.dot(mix, wg_ref[...], preferred_element_type=F32)
    tail_ref[...] = u[tm - POOL_HALO:, :]


def _proj_in(kern, a, w, layer, col0, ncols, out_dtype, *, extra=(), extra_specs=(),
             scratch=(), prefetch=None, tm=1024, tn=1024, name="proj_in"):
    t, kdim = a.shape
    jb = col0 // tn
    in_specs = [pl.BlockSpec((tm, kdim), lambda j, i, *_: (i, 0)),
                pl.BlockSpec((None, kdim, tn), lambda j, i, *_: (layer, 0, jb + j))]
    in_specs += list(extra_specs)
    out_spec = pl.BlockSpec((tm, tn), lambda j, i, *_: (i, j))
    grid = (ncols // tn, t // tm)
    out_shape = jax.ShapeDtypeStruct((t, ncols), out_dtype)
    cp = _params(("arbitrary", "arbitrary"))
    if prefetch is None:
        return pl.pallas_call(kern, out_shape=out_shape, grid=grid, in_specs=in_specs,
                              out_specs=out_spec, scratch_shapes=list(scratch),
                              compiler_params=cp, name=name)(a, w, *extra)
    gs = pltpu.PrefetchScalarGridSpec(num_scalar_prefetch=1, grid=grid, in_specs=in_specs,
                                      out_specs=out_spec, scratch_shapes=list(scratch))
    return pl.pallas_call(kern, out_shape=out_shape, grid_spec=gs,
                          compiler_params=cp, name=name)(prefetch, a, w, *extra)


def _proj_out_kernel(a_ref, w_ref, r_ref, g_ref, *o_refs, last):
    h = r_ref[...] + jnp.dot(a_ref[...], w_ref[...], preferred_element_type=F32)
    ms = jnp.mean(h * h, axis=-1, keepdims=True)
    hn = h * lax.rsqrt(ms + EPS) * g_ref[...]
    if last:
        o_refs[0][...] = hn
    else:
        o_refs[0][...] = h
        o_refs[1][...] = hn.astype(BF16)


def _proj_out(a, w, layer, resid, g, g_layer, *, last, tm=512):
    t, kdim = a.shape
    n = w.shape[2]
    row = lambda i: (i, 0)
    in_specs = [pl.BlockSpec((tm, kdim), row),
                pl.BlockSpec((None, kdim, n), lambda i: (layer, 0, 0), pipeline_mode=pl.Buffered(1)),
                pl.BlockSpec((tm, n), row),
                pl.BlockSpec((None, 1, n), lambda i: (g_layer, 0, 0))]
    if last:
        out_shape = jax.ShapeDtypeStruct((t, n), F32)
        out_specs = pl.BlockSpec((tm, n), row)
    else:
        out_shape = (jax.ShapeDtypeStruct((t, n), F32), jax.ShapeDtypeStruct((t, n), BF16))
        out_specs = (pl.BlockSpec((tm, n), row), pl.BlockSpec((tm, n), row))
    return pl.pallas_call(
        functools.partial(_proj_out_kernel, last=last),
        out_shape=out_shape,
        grid=(t // tm,),
        in_specs=in_specs,
        out_specs=out_specs,
        compiler_params=_params(("arbitrary",)),
        name="proj_out",
    )(a, w, resid, g.reshape(g.shape[0], 1, n))


def _retention_kernel(lg_ref, q_ref, k_ref, v_ref, sz_ref, gn_ref, o_ref, state_ref, sbf_ref, *, chunk, n_sub):
    h = pl.program_id(1)
    c = pl.program_id(2)

    @pl.when(c == 0)
    def _():
        state_ref[...] = jnp.zeros_like(state_ref)
        sbf_ref[...] = jnp.zeros_like(sbf_ref)

    chunk_decay = jnp.exp(jnp.full((1, 1), float(chunk), F32) * lg_ref[h])
    ii = lax.broadcasted_iota(jnp.int32, (chunk, chunk), 0)
    jj = lax.broadcasted_iota(jnp.int32, (chunk, chunk), 1)
    causal = ii >= jj
    gn = gn_ref[...]

    for s in range(n_sub):
        rows = pl.ds(s * chunk, chunk)
        q = q_ref[rows, :]
        k = k_ref[rows, :]
        v = v_ref[rows, :]
        scores = lax.dot_general(q, k, (((1,), (1,)), ((), ())), preferred_element_type=F32)
        p = jnp.where(causal, scores, 0.0).astype(BF16)
        lhs = jnp.concatenate([p, q], axis=1)
        rhs = jnp.concatenate([v, sbf_ref[...]], axis=0)
        o = jnp.dot(lhs, rhs, preferred_element_type=F32)
        upd = lax.dot_general(k, v, (((0,), (0,)), ((), ())), preferred_element_type=F32)
        state = chunk_decay * (state_ref[...] + upd)
        state_ref[...] = state
        sbf_ref[...] = state.astype(BF16)

        mu = jnp.mean(o, axis=-1, keepdims=True)
        oc = o - mu
        var = jnp.mean(oc * oc, axis=-1, keepdims=True)
        y = oc * lax.rsqrt(var + EPS) * gn
        o_ref[rows, :] = (y * sz_ref[rows, :].astype(F32)).astype(o_ref.dtype)


def _retention(qk, v, sz, lg, gn, layer, *, batch, seq, rows=2048):
    t = qk.shape[0]
    nh = N_RET_HEADS
    cps = seq // rows
    kern = functools.partial(_retention_kernel, chunk=RET_CHUNK, n_sub=rows // RET_CHUNK)
    blk = lambda b, h, c: (b * cps + c, h)
    return pl.pallas_call(
        kern,
        out_shape=jax.ShapeDtypeStruct((t, nh * HEAD_V), BF16),
        grid=(batch, nh, cps),
        in_specs=[pl.BlockSpec(memory_space=pltpu.SMEM),
                  pl.BlockSpec((rows, HEAD_QK), blk),
                  pl.BlockSpec((rows, HEAD_QK), lambda b, h, c: (b * cps + c, nh + h)),
                  pl.BlockSpec((rows, HEAD_V), blk),
                  pl.BlockSpec((rows, HEAD_V), blk),
                  pl.BlockSpec((None, 1, HEAD_V), lambda b, h, c: (layer, 0, h))],
        out_specs=pl.BlockSpec((rows, HEAD_V), blk),
        scratch_shapes=[pltpu.VMEM((HEAD_QK, HEAD_V), F32), pltpu.VMEM((HEAD_QK, HEAD_V), BF16)],
        compiler_params=_params(("arbitrary", "arbitrary", "arbitrary")),
        name="retention",
    )(lg, qk, qk, v, sz, gn.reshape(gn.shape[0], 1, nh * HEAD_V))


def _rope_tables(seq):
    inv = ROPE_BASE ** (-jnp.arange(ROPE_HALF, dtype=F32) / ROPE_HALF)
    ang = jnp.arange(seq, dtype=F32)[:, None] * inv[None, :]
    return jnp.cos(ang), jnp.sin(ang)


def _retention_layer(hn, w_in, gn, layer, cos, sin, lg, *, batch, seq, tm=1024, tn=1024):
    qk_cols = N_RET_HEADS * HEAD_QK
    v_cols = N_RET_HEADS * HEAD_V
    bps = seq // tm
    rope = functools.partial(_proj_rope_kernel, tm=tm, tn=tn, chunk=RET_CHUNK)
    tab = pl.BlockSpec((tm, ROPE_HALF), lambda j, i, *_: (i % bps, 0))
    slot_lg = jnp.concatenate([lg, -lg])
    qk = _proj_in(rope, hn, w_in, layer, 0, 2 * qk_cols, BF16, extra=(cos, sin),
                  extra_specs=(tab, tab), prefetch=slot_lg, tm=tm, tn=tn, name="proj_qk")
    v = _proj_in(_proj_plain_kernel, hn, w_in, layer, 2 * qk_cols, v_cols, BF16,
                 tm=tm, tn=tn, name="proj_v")
    sz = _proj_in(_proj_silu_kernel, hn, w_in, layer, 2 * qk_cols + v_cols, v_cols, BF16,
                  tm=tm, tn=tn, name="proj_z")
    return _retention(qk, v, sz, lg, gn, layer, batch=batch, seq=seq)


def _pooling_layer(hn, w_in, w_grp, b_grp, scale, layer, *, seq, tm=1024):
    n_layers, ng, gdim, _ = w_grp.shape
    width = ng * gdim
    assert tuple(2 << g for g in range(ng)) == POOL_WINDOWS and POOL_WINDOWS[-1] == 1 << POOL_LEVELS
    pool = functools.partial(_proj_pool_kernel, tm=tm, bps=seq // tm)
    wg_spec = pl.BlockSpec((None, None, gdim, gdim), lambda j, i: (layer, j, 0, 0))
    y = _proj_in(pool, hn, w_in, layer, 0, width, F32, extra=(w_grp,), extra_specs=(wg_spec,),
                 scratch=(pltpu.VMEM((2, POOL_HALO + tm, gdim), F32), pltpu.VMEM((POOL_HALO, gdim), F32)),
                 tm=tm, tn=gdim, name="proj_pool")
    row_vec = pl.BlockSpec((None, 1, gdim), lambda j, i: (layer, 0, j))
    return _proj_in(_proj_gate_kernel, hn, w_in, layer, width, width, BF16,
                    extra=(y, b_grp.reshape(n_layers, 1, width), scale.reshape(n_layers, 1, width)),
                    extra_specs=(pl.BlockSpec((tm, gdim), lambda j, i: (i, j)), row_vec, row_vec),
                    tm=tm, tn=gdim, name="proj_gate")


def kernel(x, ret_norm, ret_w_in, ret_gn, ret_w_out, pool_norm, pool_w_in, pool_w_grp, pool_b_grp, pool_scale, pool_w_out, final_norm):
    batch, seq, d = x.shape
    depth = ret_norm.shape[0] + pool_norm.shape[0]
    cos, sin = _rope_tables(seq)
    lg = jnp.log1p(-jnp.exp2(-5.0 - jnp.arange(N_RET_HEADS, dtype=F32)))
    ret_w_in, ret_w_out, pool_w_in, pool_w_grp, pool_w_out = (
        w.astype(BF16) for w in (ret_w_in, ret_w_out, pool_w_in, pool_w_grp, pool_w_out))
    final = final_norm.reshape(1, d)

    h = x.reshape(batch * seq, d)
    hn = _rmsnorm(h, ret_norm, 0, BF16)
    for i in range(depth):
        j = i // 2
        if i % 2 == 0:
            act = _retention_layer(hn, ret_w_in, ret_gn, j, cos, sin, lg, batch=batch, seq=seq)
            w_out = ret_w_out
        else:
            act = _pooling_layer(hn, pool_w_in, pool_w_grp, pool_b_grp, pool_scale, j, seq=seq)
            w_out = pool_w_out
        if i + 1 == depth:
            out = _proj_out(act, w_out, j, h, final, 0, last=True)
        else:
            nxt, nj = (pool_norm, j) if i % 2 == 0 else (ret_norm, j + 1)
            h, hn = _proj_out(act, w_out, j, h, nxt, nj, last=False)
    return out.reshape(batch, seq, d)
```

```python
import functools

import jax
import jax.numpy as jnp
from jax import lax
from jax.experimental import pallas as pl
from jax.experimental.pallas import tpu as pltpu

F32 = jnp.float32
BF16 = jnp.bfloat16

EPS = 1e-6
N_RET_HEADS = 8
HEAD_QK = 256
HEAD_V = 512
ROPE_HALF = HEAD_QK // 2
ROPE_BASE = 10000.0
POOL_WINDOWS = (2, 4, 8, 16)
POOL_HALO = 16
RET_CHUNK = 256

VMEM_LIMIT = 56 * 1024 * 1024


def _params(sem):
    return pltpu.CompilerParams(dimension_semantics=sem, vmem_limit_bytes=VMEM_LIMIT)


def _silu(z):
    return z * (1.0 / (1.0 + jnp.exp(-z)))


def _rmsnorm_kernel(x_ref, g_ref, o_ref):
    x = x_ref[...]
    ms = jnp.mean(x * x, axis=-1, keepdims=True)
    o_ref[...] = (x * lax.rsqrt(ms + EPS) * g_ref[...]).astype(o_ref.dtype)


def _rmsnorm(x, g, layer, out_dtype, tr=512):
    t, d = x.shape
    return pl.pallas_call(
        _rmsnorm_kernel,
        out_shape=jax.ShapeDtypeStruct((t, d), out_dtype),
        grid=(t // tr,),
        in_specs=[pl.BlockSpec((tr, d), lambda i: (i, 0)),
                  pl.BlockSpec((None, 1, d), lambda i: (layer, 0, 0))],
        out_specs=pl.BlockSpec((tr, d), lambda i: (i, 0)),
        compiler_params=_params(("arbitrary",)),
        name="rmsnorm",
    )(x, g.reshape(g.shape[0], 1, d))


def _cast_weight_tile(w_ref, wb_ref):
    @pl.when(pl.program_id(1) == 0)
    def _():
        wb_ref[...] = w_ref[...].astype(BF16)


def _proj_plain_kernel(a_ref, w_ref, o_ref, wb_ref):
    _cast_weight_tile(w_ref, wb_ref)
    acc = jnp.dot(a_ref[...], wb_ref[...], preferred_element_type=F32)
    o_ref[...] = acc.astype(o_ref.dtype)


def _proj_silu_kernel(a_ref, w_ref, o_ref, wb_ref):
    _cast_weight_tile(w_ref, wb_ref)
    acc = jnp.dot(a_ref[...], wb_ref[...], preferred_element_type=F32)
    o_ref[...] = _silu(acc).astype(o_ref.dtype)


def _proj_rope_kernel(lg_ref, a_ref, w_ref, cos_ref, sin_ref, o_ref, wb_ref, *, tm, tn, chunk):
    _cast_weight_tile(w_ref, wb_ref)
    j = pl.program_id(0)
    acc = jnp.dot(a_ref[...], wb_ref[...], preferred_element_type=F32)
    cos = cos_ref[...]
    sin = sin_ref[...]
    row = lax.broadcasted_iota(jnp.int32, (tm, 1), 0)
    pos1 = ((row % chunk) + 1).astype(F32)
    heads_per_tile = tn // HEAD_QK
    for hh in range(heads_per_tile):
        slot = j * heads_per_tile + hh
        fac = jnp.exp(pos1 * lg_ref[slot])
        fac = fac * jnp.where(slot < N_RET_HEADS, 1.0, HEAD_QK ** -0.5).astype(F32)
        cf = cos * fac
        sf = sin * fac
        a = hh * HEAD_QK
        x1 = acc[:, a:a + ROPE_HALF]
        x2 = acc[:, a + ROPE_HALF:a + HEAD_QK]
        o_ref[:, a:a + ROPE_HALF] = (x1 * cf - x2 * sf).astype(o_ref.dtype)
        o_ref[:, a + ROPE_HALF:a + HEAD_QK] = (x1 * sf + x2 * cf).astype(o_ref.dtype)


def _pool_gate_body(a_ref, w_ref, wg_ref, u_ref, uh_ref, b_ref, sc_ref, o_ref, *, levels, blk, tm):
    u = u_ref[...]
    halo = jnp.where(blk == 0, 0.0, uh_ref[...])
    s = jnp.concatenate([halo, u], axis=0)
    for lvl in range(levels):
        s = s + pltpu.roll(s, 1 << lvl, 0)
    tpos = (blk * tm + lax.broadcasted_iota(jnp.int32, (tm, 1), 0)).astype(F32)
    inv_cnt = 1.0 / jnp.minimum(tpos + 1.0, float(1 << levels))
    mix = (s[POOL_HALO:, :] * inv_cnt - u).astype(BF16)
    y = jnp.dot(mix, wg_ref[...], preferred_element_type=F32)
    z = jnp.dot(a_ref[...], w_ref[...], preferred_element_type=F32)
    o_ref[...] = ((y + b_ref[...]) * sc_ref[...] * _silu(z)).astype(o_ref.dtype)


def _pool_gate_kernel(*refs, tm, bps):
    g = pl.program_id(0)
    blk = pl.program_id(1) % bps
    for k, win in enumerate(POOL_WINDOWS):
        @pl.when(g == k)
        def _(levels=win.bit_length() - 1):
            _pool_gate_body(*refs, levels=levels, blk=blk, tm=tm)


def _proj_in(kern, a, w, layer, col0, ncols, out_dtype, *, extra=(), extra_specs=(),
             scratch=(), prefetch=None, cast=True, tm=1024, tn=1024, name="proj_in"):
    t, kdim = a.shape
    jb = col0 // tn
    in_specs = [pl.BlockSpec((tm, kdim), lambda j, i, *_: (i, 0)),
                pl.BlockSpec((None, kdim, tn), lambda j, i, *_: (layer, 0, jb + j))]
    in_specs += list(extra_specs)
    out_spec = pl.BlockSpec((tm, tn), lambda j, i, *_: (i, j))
    grid = (ncols // tn, t // tm)
    out_shape = jax.ShapeDtypeStruct((t, ncols), out_dtype)
    scratch = ([pltpu.VMEM((kdim, tn), BF16)] if cast else []) + list(scratch)
    cp = _params(("arbitrary", "arbitrary"))
    if prefetch is None:
        return pl.pallas_call(kern, out_shape=out_shape, grid=grid, in_specs=in_specs,
                              out_specs=out_spec, scratch_shapes=scratch,
                              compiler_params=cp, name=name)(a, w, *extra)
    gs = pltpu.PrefetchScalarGridSpec(num_scalar_prefetch=1, grid=grid, in_specs=in_specs,
                                      out_specs=out_spec, scratch_shapes=scratch)
    return pl.pallas_call(kern, out_shape=out_shape, grid_spec=gs,
                          compiler_params=cp, name=name)(prefetch, a, w, *extra)


def _proj_out_kernel(a_ref, w_ref, r_ref, g_ref, *o_refs, last):
    h = r_ref[...] + jnp.dot(a_ref[...], w_ref[...], preferred_element_type=F32)
    ms = jnp.mean(h * h, axis=-1, keepdims=True)
    hn = h * lax.rsqrt(ms + EPS) * g_ref[...]
    if last:
        o_refs[0][...] = hn
    else:
        o_refs[0][...] = h
        o_refs[1][...] = hn.astype(BF16)


def _proj_out(a, w, layer, resid, g, g_layer, *, last, tm=512):
    t, kdim = a.shape
    n = w.shape[2]
    row = lambda i: (i, 0)
    in_specs = [pl.BlockSpec((tm, kdim), row),
                pl.BlockSpec((None, kdim, n), lambda i: (layer, 0, 0), pipeline_mode=pl.Buffered(1)),
                pl.BlockSpec((tm, n), row),
                pl.BlockSpec((None, 1, n), lambda i: (g_layer, 0, 0))]
    if last:
        out_shape = jax.ShapeDtypeStruct((t, n), F32)
        out_specs = pl.BlockSpec((tm, n), row)
    else:
        out_shape = (jax.ShapeDtypeStruct((t, n), F32), jax.ShapeDtypeStruct((t, n), BF16))
        out_specs = (pl.BlockSpec((tm, n), row), pl.BlockSpec((tm, n), row))
    return pl.pallas_call(
        functools.partial(_proj_out_kernel, last=last),
        out_shape=out_shape,
        grid=(t // tm,),
        in_specs=in_specs,
        out_specs=out_specs,
        compiler_params=_params(("arbitrary",)),
        name="proj_out",
    )(a, w, resid, g.reshape(g.shape[0], 1, n))


def _retention_kernel(lg_ref, q_ref, k_ref, v_ref, sz_ref, gn_ref, o_ref,
                      state_ref, p_ref, upd_ref, sb_ref, *, chunk, n_sub):
    h = pl.program_id(1)
    c = pl.program_id(2)

    @pl.when(c == 0)
    def _():
        state_ref[...] = jnp.zeros_like(state_ref)

    chunk_decay = jnp.exp(jnp.full((1, 1), float(chunk), F32) * lg_ref[h])
    ii = lax.broadcasted_iota(jnp.int32, (chunk, chunk), 0)
    jj = lax.broadcasted_iota(jnp.int32, (chunk, chunk), 1)
    causal = ii >= jj
    gn = gn_ref[...]

    for s in range(n_sub):
        rows = pl.ds(s * chunk, chunk)
        k = k_ref[rows, :]
        scores = lax.dot_general(q_ref[rows, :], k, (((1,), (1,)), ((), ())),
                                 preferred_element_type=F32)
        p_ref[s] = jnp.where(causal, scores, 0.0).astype(BF16)
        upd_ref[s] = lax.dot_general(k, v_ref[rows, :], (((0,), (0,)), ((), ())),
                                     preferred_element_type=F32)

    state = state_ref[...]
    for s in range(n_sub):
        sb_ref[s] = state.astype(BF16)
        state = chunk_decay * (state + upd_ref[s])
    state_ref[...] = state

    for s in range(n_sub):
        rows = pl.ds(s * chunk, chunk)
        lhs = jnp.concatenate([p_ref[s], q_ref[rows, :]], axis=1)
        rhs = jnp.concatenate([v_ref[rows, :], sb_ref[s]], axis=0)
        o = jnp.dot(lhs, rhs, preferred_element_type=F32)

        mu = jnp.mean(o, axis=-1, keepdims=True)
        oc = o - mu
        var = jnp.mean(oc * oc, axis=-1, keepdims=True)
        y = oc * lax.rsqrt(var + EPS) * gn
        o_ref[rows, :] = (y * sz_ref[rows, :].astype(F32)).astype(o_ref.dtype)


def _retention(qk, v, sz, lg, gn, layer, *, batch, seq, rows=2048):
    t = qk.shape[0]
    nh = N_RET_HEADS
    cps = seq // rows
    n_sub = rows // RET_CHUNK
    kern = functools.partial(_retention_kernel, chunk=RET_CHUNK, n_sub=n_sub)
    blk = lambda b, h, c: (b * cps + c, h)
    return pl.pallas_call(
        kern,
        out_shape=jax.ShapeDtypeStruct((t, nh * HEAD_V), BF16),
        grid=(batch, nh, cps),
        in_specs=[pl.BlockSpec(memory_space=pltpu.SMEM),
                  pl.BlockSpec((rows, HEAD_QK), blk),
                  pl.BlockSpec((rows, HEAD_QK), lambda b, h, c: (b * cps + c, nh + h)),
                  pl.BlockSpec((rows, HEAD_V), blk),
                  pl.BlockSpec((rows, HEAD_V), blk),
                  pl.BlockSpec((None, 1, HEAD_V), lambda b, h, c: (layer, 0, h))],
        out_specs=pl.BlockSpec((rows, HEAD_V), blk),
        scratch_shapes=[pltpu.VMEM((HEAD_QK, HEAD_V), F32),
                        pltpu.VMEM((n_sub, RET_CHUNK, RET_CHUNK), BF16),
                        pltpu.VMEM((n_sub, HEAD_QK, HEAD_V), F32),
                        pltpu.VMEM((n_sub, HEAD_QK, HEAD_V), BF16)],
        compiler_params=_params(("arbitrary", "arbitrary", "arbitrary")),
        name="retention",
    )(lg, qk, qk, v, sz, gn.reshape(gn.shape[0], 1, nh * HEAD_V))


def _rope_tables(seq):
    inv = ROPE_BASE ** (-jnp.arange(ROPE_HALF, dtype=F32) / ROPE_HALF)
    ang = jnp.arange(seq, dtype=F32)[:, None] * inv[None, :]
    return jnp.cos(ang), jnp.sin(ang)


def _retention_layer(hn, w_in, gn, layer, cos, sin, lg, *, batch, seq, tm=1024, tn=1024):
    qk_cols = N_RET_HEADS * HEAD_QK
    v_cols = N_RET_HEADS * HEAD_V
    bps = seq // tm
    rope = functools.partial(_proj_rope_kernel, tm=tm, tn=tn, chunk=RET_CHUNK)
    tab = pl.BlockSpec((tm, ROPE_HALF), lambda j, i, *_: (i % bps, 0))
    slot_lg = jnp.concatenate([lg, -lg])
    qk = _proj_in(rope, hn, w_in, layer, 0, 2 * qk_cols, BF16, extra=(cos, sin),
                  extra_specs=(tab, tab), prefetch=slot_lg, tm=tm, tn=tn, name="proj_qk")
    v = _proj_in(_proj_plain_kernel, hn, w_in, layer, 2 * qk_cols, v_cols, BF16,
                 tm=tm, tn=tn, name="proj_v")
    sz = _proj_in(_proj_silu_kernel, hn, w_in, layer, 2 * qk_cols + v_cols, v_cols, BF16,
                  tm=tm, tn=tn, name="proj_z")
    return _retention(qk, v, sz, lg, gn, layer, batch=batch, seq=seq)


def _pooling_layer(hn, w_in, w_z, w_grp, b_grp, scale, layer, *, seq, tm=1024):
    n_layers, ng, gdim, _ = w_grp.shape
    width = ng * gdim
    assert ng == len(POOL_WINDOWS) and POOL_WINDOWS[-1] <= POOL_HALO
    u = _proj_in(_proj_plain_kernel, hn, w_in, layer, 0, width, F32, tm=tm, tn=gdim, name="proj_u")
    hb = tm // POOL_HALO
    row_vec = pl.BlockSpec((None, 1, gdim), lambda j, i: (layer, 0, j))
    extra_specs = (pl.BlockSpec((None, None, gdim, gdim), lambda j, i: (layer, j, 0, 0)),
                   pl.BlockSpec((tm, gdim), lambda j, i: (i, j)),
                   pl.BlockSpec((POOL_HALO, gdim), lambda j, i: (jnp.maximum(i * hb - 1, 0), j)),
                   row_vec, row_vec)
    return _proj_in(functools.partial(_pool_gate_kernel, tm=tm, bps=seq // tm),
                    hn, w_z, layer, 0, width, BF16,
                    extra=(w_grp, u, u, b_grp.reshape(n_layers, 1, width), scale.reshape(n_layers, 1, width)),
                    extra_specs=extra_specs, cast=False, tm=tm, tn=gdim, name="pool_gate")


def kernel(x, ret_norm, ret_w_in, ret_gn, ret_w_out, pool_norm, pool_w_in, pool_w_grp, pool_b_grp, pool_scale, pool_w_out, final_norm):
    batch, seq, d = x.shape
    depth = ret_norm.shape[0] + pool_norm.shape[0]
    cos, sin = _rope_tables(seq)
    lg = jnp.log1p(-jnp.exp2(-5.0 - jnp.arange(N_RET_HEADS, dtype=F32)))
    width = pool_w_grp.shape[1] * pool_w_grp.shape[2]
    pool_w_z = pool_w_in[:, :, width:].astype(BF16)
    ret_w_out, pool_w_grp, pool_w_out = (w.astype(BF16) for w in (ret_w_out, pool_w_grp, pool_w_out))
    final = final_norm.reshape(1, d)

    h = x.reshape(batch * seq, d)
    hn = _rmsnorm(h, ret_norm, 0, BF16)
    for i in range(depth):
        j = i // 2
        if i % 2 == 0:
            act = _retention_layer(hn, ret_w_in, ret_gn, j, cos, sin, lg, batch=batch, seq=seq)
            w_out = ret_w_out
        else:
            act = _pooling_layer(hn, pool_w_in, pool_w_z, pool_w_grp, pool_b_grp, pool_scale, j, seq=seq)
            w_out = pool_w_out
        if i + 1 == depth:
            out = _proj_out(act, w_out, j, h, final, 0, last=True)
        else:
            nxt, nj = (pool_norm, j) if i % 2 == 0 else (ret_norm, j + 1)
            h, hn = _proj_out(act, w_out, j, h, nxt, nj, last=False)
    return out.reshape(batch, seq, d)
```

```python
import functools

import jax
import jax.numpy as jnp
from jax import lax
from jax.experimental import pallas as pl
from jax.experimental.pallas import tpu as pltpu

F32 = jnp.float32
BF16 = jnp.bfloat16

EPS = 1e-6
N_RET_HEADS = 8
HEAD_QK = 256
HEAD_V = 512
ROPE_HALF = HEAD_QK // 2
ROPE_BASE = 10000.0
POOL_WINDOWS = (2, 4, 8, 16)
POOL_HALO = 16
RET_CHUNK = 256
MXU_ROWS = 512
OUT_SLAB_ROWS = 256

VMEM_LIMIT = 56 * 1024 * 1024


def _params(sem):
    return pltpu.CompilerParams(dimension_semantics=sem, vmem_limit_bytes=VMEM_LIMIT)


def _silu(z):
    return z * (1.0 / (1.0 + jnp.exp(-z)))


def _rmsnorm_kernel(x_ref, g_ref, o_ref):
    x = x_ref[...]
    ms = jnp.mean(x * x, axis=-1, keepdims=True)
    o_ref[...] = (x * lax.rsqrt(ms + EPS) * g_ref[...]).astype(o_ref.dtype)


def _rmsnorm(x, g, layer, out_dtype, tr=512):
    t, d = x.shape
    return pl.pallas_call(
        _rmsnorm_kernel,
        out_shape=jax.ShapeDtypeStruct((t, d), out_dtype),
        grid=(t // tr,),
        in_specs=[pl.BlockSpec((tr, d), lambda i: (i, 0)),
                  pl.BlockSpec((None, 1, d), lambda i: (layer, 0, 0))],
        out_specs=pl.BlockSpec((tr, d), lambda i: (i, 0)),
        compiler_params=_params(("arbitrary",)),
        name="rmsnorm",
    )(x, g.reshape(g.shape[0], 1, d))


def _weight_tile(w_ref, wb_ref):
    if wb_ref is None:
        return w_ref[...]

    @pl.when(pl.program_id(1) == 0)
    def _():
        wb_ref[...] = w_ref[...].astype(BF16)
    return wb_ref[...]


def _row_slabs(tm, slab=MXU_ROWS):
    return [pl.ds(r, min(slab, tm)) for r in range(0, tm, slab)]


def _proj_plain_kernel(a_ref, w_ref, o_ref, wb_ref=None):
    w = _weight_tile(w_ref, wb_ref)
    for rows in _row_slabs(a_ref.shape[0]):
        acc = jnp.dot(a_ref[rows, :], w, preferred_element_type=F32)
        o_ref[rows, :] = acc.astype(o_ref.dtype)


def _proj_silu_kernel(a_ref, w_ref, o_ref, wb_ref=None):
    w = _weight_tile(w_ref, wb_ref)
    for rows in _row_slabs(a_ref.shape[0]):
        acc = jnp.dot(a_ref[rows, :], w, preferred_element_type=F32)
        o_ref[rows, :] = _silu(acc).astype(o_ref.dtype)


def _proj_rope_kernel(lg_ref, a_ref, w_ref, cos_ref, sin_ref, o_ref, wb_ref=None, *, tn, chunk):
    j = pl.program_id(0)
    w = _weight_tile(w_ref, wb_ref)
    heads_per_tile = tn // HEAD_QK
    for rows in _row_slabs(a_ref.shape[0]):
        acc = jnp.dot(a_ref[rows, :], w, preferred_element_type=F32)
        cos = cos_ref[rows, :]
        sin = sin_ref[rows, :]
        row = rows.start + lax.broadcasted_iota(jnp.int32, (rows.size, 1), 0)
        pos1 = ((row % chunk) + 1).astype(F32)
        for hh in range(heads_per_tile):
            slot = j * heads_per_tile + hh
            fac = jnp.exp(pos1 * lg_ref[slot])
            fac = fac * jnp.where(slot < N_RET_HEADS, 1.0, HEAD_QK ** -0.5).astype(F32)
            cf = cos * fac
            sf = sin * fac
            a = hh * HEAD_QK
            x1 = acc[:, a:a + ROPE_HALF]
            x2 = acc[:, a + ROPE_HALF:a + HEAD_QK]
            o_ref[rows, a:a + ROPE_HALF] = (x1 * cf - x2 * sf).astype(o_ref.dtype)
            o_ref[rows, a + ROPE_HALF:a + HEAD_QK] = (x1 * sf + x2 * cf).astype(o_ref.dtype)


def _pool_gate_body(a_ref, w_ref, wg_ref, u_ref, uh_ref, b_ref, sc_ref, o_ref, *, levels, blk, tm):
    halo = jnp.where(blk == 0, 0.0, uh_ref[...])
    for rows in _row_slabs(tm):
        z = jnp.dot(a_ref[rows, :], w_ref[...], preferred_element_type=F32)
        u = u_ref[rows, :]
        if rows.start:
            halo = u_ref[pl.ds(rows.start - POOL_HALO, POOL_HALO), :]
        s = jnp.concatenate([halo, u], axis=0)
        for lvl in range(levels):
            s = s + pltpu.roll(s, 1 << lvl, 0)
        tpos = (blk * tm + rows.start + lax.broadcasted_iota(jnp.int32, (rows.size, 1), 0)).astype(F32)
        inv_cnt = 1.0 / jnp.minimum(tpos + 1.0, float(1 << levels))
        mix = (s[POOL_HALO:, :] * inv_cnt - u).astype(BF16)
        y = jnp.dot(mix, wg_ref[...], preferred_element_type=F32)
        o_ref[rows, :] = ((y + b_ref[...]) * sc_ref[...] * _silu(z)).astype(o_ref.dtype)


def _pool_gate_kernel(*refs, tm, bps):
    g = pl.program_id(0)
    blk = pl.program_id(1) % bps
    for k, win in enumerate(POOL_WINDOWS):
        @pl.when(g == k)
        def _(levels=win.bit_length() - 1):
            _pool_gate_body(*refs, levels=levels, blk=blk, tm=tm)


def _with_side_casts(body, n_pre, n_in, n_cast):
    if n_cast == 0:
        return body

    def kern(*refs):
        head = n_pre + n_in
        slabs_in = refs[head:head + n_cast]
        slabs_out = refs[head + n_cast + 1:head + 2 * n_cast + 1]
        for src, dst in zip(slabs_in, slabs_out):
            dst[...] = src[...].astype(BF16)
        body(*refs[:head], refs[head + n_cast], *refs[head + 2 * n_cast + 1:])
    return kern


def _proj_in(body, a, w, layer, col0, ncols, out_dtype, *, extra=(), extra_specs=(),
             prefetch=None, casts=(), tm, tn=1024, name):
    t, kdim = a.shape
    jb = col0 // tn
    nb = t // tm
    grid = (ncols // tn, nb)
    steps = grid[0] * nb
    step = lambda j, i: j * nb + i
    in_specs = [pl.BlockSpec((tm, kdim), lambda j, i, *_: (i, 0)),
                pl.BlockSpec((None, kdim, tn), lambda j, i, *_: (layer, 0, jb + j))]
    in_specs += list(extra_specs)
    out_specs = [pl.BlockSpec((tm, tn), lambda j, i, *_: (i, j))]
    out_shape = [jax.ShapeDtypeStruct((t, ncols), out_dtype)]
    for src, src_layer, cb, width in casts:
        rb = src.shape[1] // steps
        in_specs.append(pl.BlockSpec((None, rb, width),
                                     lambda j, i, *_, l=src_layer, c=cb: (l, step(j, i), c)))
        out_specs.append(pl.BlockSpec((rb, width), lambda j, i, *_: (step(j, i), 0)))
        out_shape.append(jax.ShapeDtypeStruct((src.shape[1], width), BF16))
    scratch = [pltpu.VMEM((kdim, tn), BF16)] if w.dtype == F32 else []
    n_pre = 0 if prefetch is None else 1
    kern = _with_side_casts(body, n_pre, 2 + len(extra), len(casts))
    gs = pltpu.PrefetchScalarGridSpec(num_scalar_prefetch=n_pre, grid=grid, in_specs=in_specs,
                                      out_specs=out_specs, scratch_shapes=scratch)
    args = ([] if prefetch is None else [prefetch]) + [a, w, *extra] + [c[0] for c in casts]
    outs = pl.pallas_call(kern, out_shape=out_shape, grid_spec=gs,
                          compiler_params=_params(("arbitrary", "arbitrary")), name=name)(*args)
    return outs[0], list(outs[1:])


def _proj_out_kernel(a_ref, w_ref, r_ref, g_ref, *o_refs, last):
    for rows in _row_slabs(a_ref.shape[0], OUT_SLAB_ROWS):
        h = r_ref[rows, :] + jnp.dot(a_ref[rows, :], w_ref[...], preferred_element_type=F32)
        ms = jnp.mean(h * h, axis=-1, keepdims=True)
        hn = h * lax.rsqrt(ms + EPS) * g_ref[...]
        if last:
            o_refs[0][rows, :] = hn
        else:
            o_refs[0][rows, :] = h
            o_refs[1][rows, :] = hn.astype(BF16)


def _proj_out(a, w, resid, g, g_layer, *, last, tm=512):
    t, kdim = a.shape
    n = w.shape[1]
    row = lambda i: (i, 0)
    in_specs = [pl.BlockSpec((tm, kdim), row),
                pl.BlockSpec((kdim, n), lambda i: (0, 0), pipeline_mode=pl.Buffered(1)),
                pl.BlockSpec((tm, n), row),
                pl.BlockSpec((None, 1, n), lambda i: (g_layer, 0, 0))]
    if last:
        out_shape = jax.ShapeDtypeStruct((t, n), F32)
        out_specs = pl.BlockSpec((tm, n), row)
    else:
        out_shape = (jax.ShapeDtypeStruct((t, n), F32), jax.ShapeDtypeStruct((t, n), BF16))
        out_specs = (pl.BlockSpec((tm, n), row), pl.BlockSpec((tm, n), row))
    return pl.pallas_call(
        functools.partial(_proj_out_kernel, last=last),
        out_shape=out_shape,
        grid=(t // tm,),
        in_specs=in_specs,
        out_specs=out_specs,
        compiler_params=_params(("arbitrary",)),
        name="proj_out",
    )(a, w, resid, g.reshape(g.shape[0], 1, n))


def _retention_kernel(lg_ref, q_ref, k_ref, v_ref, sz_ref, gn_ref, o_ref,
                      state_ref, p_ref, upd_ref, sb_ref, *, chunk, n_sub):
    h = pl.program_id(1)
    c = pl.program_id(2)

    @pl.when(c == 0)
    def _():
        state_ref[...] = jnp.zeros_like(state_ref)

    chunk_decay = jnp.exp(jnp.full((1, 1), float(chunk), F32) * lg_ref[h])
    ii = lax.broadcasted_iota(jnp.int32, (chunk, chunk), 0)
    jj = lax.broadcasted_iota(jnp.int32, (chunk, chunk), 1)
    causal = ii >= jj
    gn = gn_ref[...]

    for s in range(n_sub):
        rows = pl.ds(s * chunk, chunk)
        k = k_ref[rows, :]
        scores = lax.dot_general(q_ref[rows, :], k, (((1,), (1,)), ((), ())),
                                 preferred_element_type=F32)
        p_ref[s] = jnp.where(causal, scores, 0.0).astype(BF16)
        upd_ref[s] = lax.dot_general(k, v_ref[rows, :], (((0,), (0,)), ((), ())),
                                     preferred_element_type=F32)

    state = state_ref[...]
    for s in range(n_sub):
        sb_ref[s] = state.astype(BF16)
        state = chunk_decay * (state + upd_ref[s])
    state_ref[...] = state

    for s in range(n_sub):
        rows = pl.ds(s * chunk, chunk)
        lhs = jnp.concatenate([p_ref[s], q_ref[rows, :]], axis=1)
        rhs = jnp.concatenate([v_ref[rows, :], sb_ref[s]], axis=0)
        o = jnp.dot(lhs, rhs, preferred_element_type=F32)

        mu = jnp.mean(o, axis=-1, keepdims=True)
        oc = o - mu
        var = jnp.mean(oc * oc, axis=-1, keepdims=True)
        y = oc * lax.rsqrt(var + EPS) * gn
        o_ref[rows, :] = (y * sz_ref[rows, :].astype(F32)).astype(o_ref.dtype)


def _retention(qk, v, sz, lg, gn, layer, *, batch, seq, rows=4096):
    t = qk.shape[0]
    nh = N_RET_HEADS
    rows = min(rows, seq)
    cps = seq // rows
    n_sub = rows // RET_CHUNK
    kern = functools.partial(_retention_kernel, chunk=RET_CHUNK, n_sub=n_sub)
    blk = lambda b, h, c: (b * cps + c, h)
    return pl.pallas_call(
        kern,
        out_shape=jax.ShapeDtypeStruct((t, nh * HEAD_V), BF16),
        grid=(batch, nh, cps),
        in_specs=[pl.BlockSpec(memory_space=pltpu.SMEM),
                  pl.BlockSpec((rows, HEAD_QK), blk),
                  pl.BlockSpec((rows, HEAD_QK), lambda b, h, c: (b * cps + c, nh + h)),
                  pl.BlockSpec((rows, HEAD_V), blk),
                  pl.BlockSpec((rows, HEAD_V), blk),
                  pl.BlockSpec((None, 1, HEAD_V), lambda b, h, c: (layer, 0, h))],
        out_specs=pl.BlockSpec((rows, HEAD_V), blk),
        scratch_shapes=[pltpu.VMEM((HEAD_QK, HEAD_V), F32),
                        pltpu.VMEM((n_sub, RET_CHUNK, RET_CHUNK), BF16),
                        pltpu.VMEM((n_sub, HEAD_QK, HEAD_V), F32),
                        pltpu.VMEM((n_sub, HEAD_QK, HEAD_V), BF16)],
        compiler_params=_params(("arbitrary", "arbitrary", "arbitrary")),
        name="retention",
    )(lg, qk, qk, v, sz, gn.reshape(gn.shape[0], 1, nh * HEAD_V))


def _rope_tables(seq):
    inv = ROPE_BASE ** (-jnp.arange(ROPE_HALF, dtype=F32) / ROPE_HALF)
    ang = jnp.arange(seq, dtype=F32)[:, None] * inv[None, :]
    return jnp.cos(ang), jnp.sin(ang)


def _retention_layer(hn, w_in, in_layer, gn, layer, cos, sin, lg, casts, *, batch, seq, tm):
    qk_cols = N_RET_HEADS * HEAD_QK
    v_cols = N_RET_HEADS * HEAD_V
    own = w_in.dtype == F32
    tm_qk = tm // 2 if own else tm
    bps = seq // tm_qk
    qk_casts = list(casts[0])
    if own:
        qk_casts = [(w_in, in_layer, 1, v_cols), (w_in, in_layer, 2, v_cols)] + qk_casts
    rope = functools.partial(_proj_rope_kernel, tn=1024, chunk=RET_CHUNK)
    tab = pl.BlockSpec((tm_qk, ROPE_HALF), lambda j, i, *_: (i % bps, 0))
    qk, c0 = _proj_in(rope, hn, w_in, in_layer if own else 0, 0, 2 * qk_cols, BF16,
                      extra=(cos, sin), extra_specs=(tab, tab),
                      prefetch=jnp.concatenate([lg, -lg]), casts=qk_casts, tm=tm_qk, name="proj_qk")
    if own:
        (w_v, v0), (w_z, z0), c0 = (c0[0][None], 0), (c0[1][None], 0), c0[2:]
    else:
        (w_v, v0), (w_z, z0) = (w_in, 2 * qk_cols), (w_in, 2 * qk_cols + v_cols)
    v, c1 = _proj_in(_proj_plain_kernel, hn, w_v, 0, v0, v_cols, BF16,
                     casts=casts[1], tm=tm, name="proj_v")
    sz, c2 = _proj_in(_proj_silu_kernel, hn, w_z, 0, z0, v_cols, BF16,
                      casts=casts[2], tm=tm, name="proj_z")
    return _retention(qk, v, sz, lg, gn, layer, batch=batch, seq=seq), c0 + c1 + c2


def _pooling_layer(hn, w_in, w_grp, b_grp, scale, layer, gate_casts, *, seq, tm):
    _, ng, gdim, _ = w_grp.shape
    n_layers = b_grp.shape[0]
    width = ng * gdim
    assert ng == len(POOL_WINDOWS) and POOL_WINDOWS[-1] <= POOL_HALO
    u, _ = _proj_in(_proj_plain_kernel, hn, w_in, 0, 0, width, F32, tm=tm, tn=gdim, name="proj_u")
    tg = tm // 2
    hb = tg // POOL_HALO
    row_vec = pl.BlockSpec((None, 1, gdim), lambda j, i: (layer, 0, j))
    extra_specs = (pl.BlockSpec((None, None, gdim, gdim), lambda j, i: (0, j, 0, 0)),
                   pl.BlockSpec((tg, gdim), lambda j, i: (i, j)),
                   pl.BlockSpec((POOL_HALO, gdim), lambda j, i: (jnp.maximum(i * hb - 1, 0), j)),
                   row_vec, row_vec)
    return _proj_in(functools.partial(_pool_gate_kernel, tm=tg, bps=seq // tg),
                    hn, w_in, 0, width, width, BF16,
                    extra=(w_grp, u, u, b_grp.reshape(n_layers, 1, width), scale.reshape(n_layers, 1, width)),
                    extra_specs=extra_specs, casts=gate_casts, tm=tg, tn=gdim, name="pool_gate")


def _whole(src, layer):
    return (src, layer, 0, src.shape[2])


def kernel(x, ret_norm, ret_w_in, ret_gn, ret_w_out, pool_norm, pool_w_in, pool_w_grp, pool_b_grp, pool_scale, pool_w_out, final_norm):
    batch, seq, d = x.shape
    n_ret, n_pool = ret_norm.shape[0], pool_norm.shape[0]
    depth = n_ret + n_pool
    tm = 2048
    _, ng, gdim, _ = pool_w_grp.shape
    grp_rows = pool_w_grp.reshape(n_pool, ng * gdim, gdim)
    cos, sin = _rope_tables(seq)
    lg = jnp.log1p(-jnp.exp2(-5.0 - jnp.arange(N_RET_HEADS, dtype=F32)))

    h = x.reshape(batch * seq, d)
    hn = _rmsnorm(h, ret_norm, 0, BF16)
    ret_in = None
    pool = None
    for i in range(depth):
        j = i // 2
        if i % 2 == 0:
            has_pool = j < n_pool
            casts = [[_whole(ret_w_out, j)],
                     [_whole(pool_w_in, j)] if has_pool else [],
                     [_whole(grp_rows, j), _whole(pool_w_out, j)] if has_pool else []]
            act, got = _retention_layer(hn, ret_w_in if ret_in is None else ret_in, j, ret_gn, j,
                                        cos, sin, lg, casts, batch=batch, seq=seq, tm=tm)
            w_out = got[0]
            pool = tuple(got[1:]) if has_pool else None
        else:
            has_ret = j + 1 < n_ret
            act, got = _pooling_layer(hn, pool[0][None], pool[1].reshape(1, ng, gdim, gdim),
                                      pool_b_grp, pool_scale, j,
                                      [_whole(ret_w_in, j + 1)] if has_ret else [], seq=seq, tm=tm)
            w_out = pool[2]
            ret_in = got[0][None] if has_ret else None
        if i + 1 == depth:
            out = _proj_out(act, w_out, h, final_norm.reshape(1, d), 0, last=True)
        else:
            nxt, nj = (pool_norm, j) if i % 2 == 0 else (ret_norm, j + 1)
            h, hn = _proj_out(act, w_out, h, nxt, nj, last=False)
    return out.reshape(batch, seq, d)
```

```python
import functools

import jax
import jax.numpy as jnp
from jax import lax
from jax.experimental import pallas as pl
from jax.experimental.pallas import tpu as pltpu

F32 = jnp.float32
BF16 = jnp.bfloat16

EPS = 1e-6
N_RET_HEADS = 8
HEAD_QK = 256
HEAD_V = 512
ROPE_HALF = HEAD_QK // 2
ROPE_BASE = 10000.0
POOL_WINDOWS = (2, 4, 8, 16)
POOL_HALO = 16
RET_CHUNK = 256
MXU_ROWS = 512
EPILOGUE_ROWS = 256

VMEM_LIMIT = 56 * 1024 * 1024


def _params(sem):
    return pltpu.CompilerParams(dimension_semantics=sem, vmem_limit_bytes=VMEM_LIMIT)


def _silu(z):
    return z * (1.0 / (1.0 + jnp.exp(-z)))


def _row_slabs(tm, slab):
    return [pl.ds(r, min(slab, tm)) for r in range(0, tm, slab)]


def _whole(src, layer):
    return (src, layer, 0, src.shape[2])


def _cast_specs(casts, steps, step_of):
    n = 1 << (steps.bit_length() - 1)
    in_specs, out_specs, out_shape = [], [], []
    for src, layer, cb, width in casts:
        rb = src.shape[1] // n
        slab = lambda *g: jnp.minimum(step_of(*g), n - 1)
        in_specs.append(pl.BlockSpec((None, rb, width), lambda *g, l=layer, c=cb: (l, slab(*g), c)))
        out_specs.append(pl.BlockSpec((rb, width), lambda *g: (slab(*g), 0)))
        out_shape.append(jax.ShapeDtypeStruct((src.shape[1], width), BF16))
    return in_specs, out_specs, out_shape


def _with_side_casts(body, n_pre, n_in, n_cast):
    if n_cast == 0:
        return body

    def kern(*refs):
        head = n_pre + n_in
        slabs_in = refs[head:head + n_cast]
        slabs_out = refs[head + n_cast + 1:head + 2 * n_cast + 1]
        for src, dst in zip(slabs_in, slabs_out):
            dst[...] = src[...].astype(BF16)
        body(*refs[:head], refs[head + n_cast], *refs[head + 2 * n_cast + 1:])
    return kern


def _rmsnorm_kernel(x_ref, g_ref, o_ref):
    x = x_ref[...]
    ms = jnp.mean(x * x, axis=-1, keepdims=True)
    o_ref[...] = (x * lax.rsqrt(ms + EPS) * g_ref[...]).astype(o_ref.dtype)


def _rmsnorm(x, g, layer, casts, tr=512):
    t, d = x.shape
    steps = t // tr
    c_in, c_out, c_shape = _cast_specs(casts, steps, lambda i: i)
    outs = pl.pallas_call(
        _with_side_casts(_rmsnorm_kernel, 0, 2, len(casts)),
        out_shape=[jax.ShapeDtypeStruct((t, d), BF16)] + c_shape,
        grid=(steps,),
        in_specs=[pl.BlockSpec((tr, d), lambda i: (i, 0)),
                  pl.BlockSpec((None, 1, d), lambda i: (layer, 0, 0))] + c_in,
        out_specs=[pl.BlockSpec((tr, d), lambda i: (i, 0))] + c_out,
        compiler_params=_params(("arbitrary",)),
        name="rmsnorm",
    )(x, g.reshape(g.shape[0], 1, d), *[c[0] for c in casts])
    return outs[0], list(outs[1:])


def _plain_body(a_ref, w_ref, o_ref):
    for rows in _row_slabs(a_ref.shape[0], MXU_ROWS):
        acc = jnp.dot(a_ref[rows, :], w_ref[...], preferred_element_type=F32)
        o_ref[rows, :] = acc.astype(o_ref.dtype)


def _silu_body(a_ref, w_ref, o_ref):
    for rows in _row_slabs(a_ref.shape[0], EPILOGUE_ROWS):
        acc = jnp.dot(a_ref[rows, :], w_ref[...], preferred_element_type=F32)
        o_ref[rows, :] = _silu(acc).astype(o_ref.dtype)


def _rope_body(lg_ref, a_ref, w_ref, cos_ref, sin_ref, o_ref, *, chunk):
    j = pl.program_id(0)
    heads_per_tile = o_ref.shape[1] // HEAD_QK
    for rows in _row_slabs(a_ref.shape[0], EPILOGUE_ROWS):
        acc = jnp.dot(a_ref[rows, :], w_ref[...], preferred_element_type=F32)
        cos = cos_ref[rows, :]
        sin = sin_ref[rows, :]
        row = rows.start + lax.broadcasted_iota(jnp.int32, (rows.size, 1), 0)
        pos1 = ((row % chunk) + 1).astype(F32)
        for hh in range(heads_per_tile):
            slot = j * heads_per_tile + hh
            fac = jnp.exp(pos1 * lg_ref[slot])
            fac = fac * jnp.where(slot < N_RET_HEADS, 1.0, HEAD_QK ** -0.5).astype(F32)
            cf = cos * fac
            sf = sin * fac
            a = hh * HEAD_QK
            x1 = acc[:, a:a + ROPE_HALF]
            x2 = acc[:, a + ROPE_HALF:a + HEAD_QK]
            o_ref[rows, a:a + ROPE_HALF] = (x1 * cf - x2 * sf).astype(o_ref.dtype)
            o_ref[rows, a + ROPE_HALF:a + HEAD_QK] = (x1 * sf + x2 * cf).astype(o_ref.dtype)


def _ret_proj_kernel(lg_ref, a_ref, w_ref, cos_ref, sin_ref, o_ref, *, chunk, n_rope, n_plain):
    j = pl.program_id(0)

    @pl.when(j < n_rope)
    def _():
        _rope_body(lg_ref, a_ref, w_ref, cos_ref, sin_ref, o_ref, chunk=chunk)

    @pl.when(jnp.logical_and(j >= n_rope, j < n_rope + n_plain))
    def _():
        _plain_body(a_ref, w_ref, o_ref)

    @pl.when(j >= n_rope + n_plain)
    def _():
        _silu_body(a_ref, w_ref, o_ref)


def _pool_gate_body(a_ref, w_ref, wg_ref, u_ref, uh_ref, b_ref, sc_ref, o_ref, *, levels, blk, tm):
    u = u_ref[...]
    halo = jnp.where(blk == 0, 0.0, uh_ref[...])
    s = jnp.concatenate([halo, u], axis=0)
    for lvl in range(levels):
        s = s + pltpu.roll(s, 1 << lvl, 0)
    tpos = (blk * tm + lax.broadcasted_iota(jnp.int32, (tm, 1), 0)).astype(F32)
    inv_cnt = 1.0 / jnp.minimum(tpos + 1.0, float(1 << levels))
    mix = (s[POOL_HALO:, :] * inv_cnt - u).astype(BF16)
    y = jnp.dot(mix, wg_ref[...], preferred_element_type=F32)
    z = jnp.dot(a_ref[...], w_ref[...], preferred_element_type=F32)
    o_ref[...] = ((y + b_ref[...]) * sc_ref[...] * _silu(z)).astype(o_ref.dtype)


def _pool_gate_kernel(*refs, tm, bps):
    g = pl.program_id(0)
    blk = pl.program_id(1) % bps
    for k, win in enumerate(POOL_WINDOWS):
        @pl.when(g == k)
        def _(levels=win.bit_length() - 1):
            _pool_gate_body(*refs, levels=levels, blk=blk, tm=tm)


def _proj_in(body, a, w, col0, ncols, out_dtype, *, extra=(), extra_specs=(),
             prefetch=None, casts=(), tm, tn=1024, name):
    t, kdim = a.shape
    jb = col0 // tn
    nb = t // tm
    grid = (ncols // tn, nb)
    c_in, c_out, c_shape = _cast_specs(casts, grid[0] * nb, lambda j, i, *_: j * nb + i)
    in_specs = [pl.BlockSpec((tm, kdim), lambda j, i, *_: (i, 0)),
                pl.BlockSpec((None, kdim, tn), lambda j, i, *_: (0, 0, jb + j))]
    in_specs += list(extra_specs) + c_in
    out_specs = [pl.BlockSpec((tm, tn), lambda j, i, *_: (i, j))] + c_out
    out_shape = [jax.ShapeDtypeStruct((t, ncols), out_dtype)] + c_shape
    n_pre = 0 if prefetch is None else 1
    kern = _with_side_casts(body, n_pre, 2 + len(extra), len(casts))
    gs = pltpu.PrefetchScalarGridSpec(num_scalar_prefetch=n_pre, grid=grid,
                                      in_specs=in_specs, out_specs=out_specs)
    args = ([] if prefetch is None else [prefetch]) + [a, w, *extra] + [c[0] for c in casts]
    outs = pl.pallas_call(kern, out_shape=out_shape, grid_spec=gs,
                          compiler_params=_params(("arbitrary", "arbitrary")), name=name)(*args)
    return outs[0], list(outs[1:])


def _proj_out_kernel(a_ref, w_ref, r_ref, g_ref, *o_refs, last):
    h = r_ref[...] + jnp.dot(a_ref[...], w_ref[...], preferred_element_type=F32)
    ms = jnp.mean(h * h, axis=-1, keepdims=True)
    hn = h * lax.rsqrt(ms + EPS) * g_ref[...]
    if last:
        o_refs[0][...] = hn
    else:
        o_refs[0][...] = h
        o_refs[1][...] = hn.astype(BF16)


def _proj_out(a, w, resid, g, g_layer, *, last, tm=512):
    t, kdim = a.shape
    n = w.shape[1]
    row = lambda i: (i, 0)
    in_specs = [pl.BlockSpec((tm, kdim), row),
                pl.BlockSpec((kdim, n), lambda i: (0, 0), pipeline_mode=pl.Buffered(1)),
                pl.BlockSpec((tm, n), row),
                pl.BlockSpec((None, 1, n), lambda i: (g_layer, 0, 0))]
    if last:
        out_shape = jax.ShapeDtypeStruct((t, n), F32)
        out_specs = pl.BlockSpec((tm, n), row)
    else:
        out_shape = (jax.ShapeDtypeStruct((t, n), F32), jax.ShapeDtypeStruct((t, n), BF16))
        out_specs = (pl.BlockSpec((tm, n), row), pl.BlockSpec((tm, n), row))
    return pl.pallas_call(
        functools.partial(_proj_out_kernel, last=last),
        out_shape=out_shape,
        grid=(t // tm,),
        in_specs=in_specs,
        out_specs=out_specs,
        compiler_params=_params(("arbitrary",)),
        name="proj_out",
    )(a, w, resid, g.reshape(g.shape[0], 1, n))


def _retention_kernel(lg_ref, q_ref, k_ref, v_ref, sz_ref, gn_ref, o_ref,
                      state_ref, p_ref, upd_ref, sb_ref, *, chunk, n_sub):
    h = pl.program_id(1)
    c = pl.program_id(2)

    @pl.when(c == 0)
    def _():
        state_ref[...] = jnp.zeros_like(state_ref)

    chunk_decay = jnp.exp(jnp.full((1, 1), float(chunk), F32) * lg_ref[h])
    ii = lax.broadcasted_iota(jnp.int32, (chunk, chunk), 0)
    jj = lax.broadcasted_iota(jnp.int32, (chunk, chunk), 1)
    causal = ii >= jj
    gn = gn_ref[...]

    for s in range(n_sub):
        rows = pl.ds(s * chunk, chunk)
        k = k_ref[rows, :]
        scores = lax.dot_general(q_ref[rows, :], k, (((1,), (1,)), ((), ())),
                                 preferred_element_type=F32)
        p_ref[s] = jnp.where(causal, scores, 0.0).astype(BF16)
        upd_ref[s] = lax.dot_general(k, v_ref[rows, :], (((0,), (0,)), ((), ())),
                                     preferred_element_type=F32)

    state = state_ref[...]
    for s in range(n_sub):
        sb_ref[s] = state.astype(BF16)
        state = chunk_decay * (state + upd_ref[s])
    state_ref[...] = state

    for s in range(n_sub):
        rows = pl.ds(s * chunk, chunk)
        lhs = jnp.concatenate([p_ref[s], q_ref[rows, :]], axis=1)
        rhs = jnp.concatenate([v_ref[rows, :], sb_ref[s]], axis=0)
        o = jnp.dot(lhs, rhs, preferred_element_type=F32)

        mu = jnp.mean(o, axis=-1, keepdims=True)
        oc = o - mu
        var = jnp.mean(oc * oc, axis=-1, keepdims=True)
        y = oc * lax.rsqrt(var + EPS) * gn
        o_ref[rows, :] = (y * sz_ref[rows, :].astype(F32)).astype(o_ref.dtype)


def _retention(p, lg, gn, layer, *, batch, seq, rows=4096):
    t = p.shape[0]
    nh = N_RET_HEADS
    rows = min(rows, seq)
    cps = seq // rows
    n_sub = rows // RET_CHUNK
    kern = functools.partial(_retention_kernel, chunk=RET_CHUNK, n_sub=n_sub)
    v_blk = 2 * nh * HEAD_QK // HEAD_V

    def blk(first):
        return lambda b, h, c: (b * cps + c, first + h)

    return pl.pallas_call(
        kern,
        out_shape=jax.ShapeDtypeStruct((t, nh * HEAD_V), BF16),
        grid=(batch, nh, cps),
        in_specs=[pl.BlockSpec(memory_space=pltpu.SMEM),
                  pl.BlockSpec((rows, HEAD_QK), blk(0)),
                  pl.BlockSpec((rows, HEAD_QK), blk(nh)),
                  pl.BlockSpec((rows, HEAD_V), blk(v_blk)),
                  pl.BlockSpec((rows, HEAD_V), blk(v_blk + nh)),
                  pl.BlockSpec((None, 1, HEAD_V), lambda b, h, c: (layer, 0, h))],
        out_specs=pl.BlockSpec((rows, HEAD_V), blk(0)),
        scratch_shapes=[pltpu.VMEM((HEAD_QK, HEAD_V), F32),
                        pltpu.VMEM((n_sub, RET_CHUNK, RET_CHUNK), BF16),
                        pltpu.VMEM((n_sub, HEAD_QK, HEAD_V), F32),
                        pltpu.VMEM((n_sub, HEAD_QK, HEAD_V), BF16)],
        compiler_params=_params(("arbitrary", "arbitrary", "arbitrary")),
        name="retention",
    )(lg, p, p, p, p, gn.reshape(gn.shape[0], 1, nh * HEAD_V))


def _rope_tables(seq):
    inv = ROPE_BASE ** (-jnp.arange(ROPE_HALF, dtype=F32) / ROPE_HALF)
    ang = jnp.arange(seq, dtype=F32)[:, None] * inv[None, :]
    return jnp.cos(ang), jnp.sin(ang)


def _retention_layer(hn, w_in, gn, layer, cos, sin, lg, casts, *, batch, seq, tm, tn=1024):
    qk_cols = 2 * N_RET_HEADS * HEAD_QK
    v_cols = N_RET_HEADS * HEAD_V
    bps = seq // tm
    kern = functools.partial(_ret_proj_kernel, chunk=RET_CHUNK,
                             n_rope=qk_cols // tn, n_plain=v_cols // tn)
    tab = pl.BlockSpec((tm, ROPE_HALF), lambda j, i, *_: (i % bps, 0))
    p, got = _proj_in(kern, hn, w_in, 0, qk_cols + 2 * v_cols, BF16, extra=(cos, sin),
                      extra_specs=(tab, tab), prefetch=jnp.concatenate([lg, -lg]),
                      casts=casts, tm=tm, tn=tn, name="proj_ret")
    return _retention(p, lg, gn, layer, batch=batch, seq=seq), got


def _pooling_layer(hn, w_in, w_grp, b_grp, scale, layer, gate_casts, *, seq, tm):
    _, ng, gdim, _ = w_grp.shape
    n_layers = b_grp.shape[0]
    width = ng * gdim
    assert ng == len(POOL_WINDOWS) and POOL_WINDOWS[-1] <= POOL_HALO
    u, _ = _proj_in(_plain_body, hn, w_in, 0, width, F32, tm=tm, tn=gdim, name="proj_u")
    tg = tm // 2
    hb = tg // POOL_HALO
    row_vec = pl.BlockSpec((None, 1, gdim), lambda j, i: (layer, 0, j))
    extra_specs = (pl.BlockSpec((None, None, gdim, gdim), lambda j, i: (0, j, 0, 0)),
                   pl.BlockSpec((tg, gdim), lambda j, i: (i, j)),
                   pl.BlockSpec((POOL_HALO, gdim), lambda j, i: (jnp.maximum(i * hb - 1, 0), j)),
                   row_vec, row_vec)
    return _proj_in(functools.partial(_pool_gate_kernel, tm=tg, bps=seq // tg),
                    hn, w_in, width, width, BF16,
                    extra=(w_grp, u, u, b_grp.reshape(n_layers, 1, width), scale.reshape(n_layers, 1, width)),
                    extra_specs=extra_specs, casts=gate_casts, tm=tg, tn=gdim, name="pool_gate")


def kernel(x, ret_norm, ret_w_in, ret_gn, ret_w_out, pool_norm, pool_w_in, pool_w_grp, pool_b_grp, pool_scale, pool_w_out, final_norm):
    batch, seq, d = x.shape
    n_ret, n_pool = ret_norm.shape[0], pool_norm.shape[0]
    depth = n_ret + n_pool
    tm = 2048
    _, ng, gdim, _ = pool_w_grp.shape
    grp_rows = pool_w_grp.reshape(n_pool, ng * gdim, gdim)
    cos, sin = _rope_tables(seq)
    lg = jnp.log1p(-jnp.exp2(-5.0 - jnp.arange(N_RET_HEADS, dtype=F32)))

    h = x.reshape(batch * seq, d)
    hn, (ret_in,) = _rmsnorm(h, ret_norm, 0, [_whole(ret_w_in, 0)])
    pool = None
    for i in range(depth):
        j = i // 2
        if i % 2 == 0:
            has_pool = j < n_pool
            casts = [_whole(ret_w_out, j)]
            if has_pool:
                casts += [_whole(pool_w_in, j), _whole(grp_rows, j), _whole(pool_w_out, j)]
            act, got = _retention_layer(hn, ret_in[None], ret_gn, j, cos, sin, lg, casts,
                                        batch=batch, seq=seq, tm=tm)
            w_out = got[0]
            pool = tuple(got[1:]) if has_pool else None
        else:
            has_ret = j + 1 < n_ret
            act, got = _pooling_layer(hn, pool[0][None], pool[1].reshape(1, ng, gdim, gdim),
                                      pool_b_grp, pool_scale, j,
                                      [_whole(ret_w_in, j + 1)] if has_ret else [], seq=seq, tm=tm)
            w_out = pool[2]
            ret_in = got[0] if has_ret else None
        if i + 1 == depth:
            out = _proj_out(act, w_out, h, final_norm.reshape(1, d), 0, last=True)
        else:
            nxt, nj = (pool_norm, j) if i % 2 == 0 else (ret_norm, j + 1)
            h, hn = _proj_out(act, w_out, h, nxt, nj, last=False)
    return out.reshape(batch, seq, d)
```

```python
import functools

import jax
import jax.numpy as jnp
from jax import lax
from jax.experimental import pallas as pl
from jax.experimental.pallas import tpu as pltpu

F32 = jnp.float32
BF16 = jnp.bfloat16

EPS = 1e-6
LOG2_E = 1.4426950408889634
N_RET_HEADS = 8
HEAD_QK = 256
HEAD_V = 512
ROPE_HALF = HEAD_QK // 2
ROPE_BASE = 10000.0
POOL_WINDOWS = (2, 4, 8, 16)
POOL_HALO = 16
RET_CHUNK = 256
MXU_ROWS = 512
EPILOGUE_ROWS = 256

VMEM_LIMIT = 56 * 1024 * 1024


def _params(sem):
    return pltpu.CompilerParams(dimension_semantics=sem, vmem_limit_bytes=VMEM_LIMIT)


def _silu(z):
    return z * (1.0 / (1.0 + jnp.exp2(z * (-LOG2_E))))


def _row_slabs(tm, slab):
    return [pl.ds(r, min(slab, tm)) for r in range(0, tm, slab)]


def _whole(src, layer):
    return (src, layer, 0, src.shape[2])


def _cast_specs(casts, steps, step_of):
    n = 1 << (steps.bit_length() - 1)
    in_specs, out_specs, out_shape = [], [], []
    for src, layer, cb, width in casts:
        rb = src.shape[1] // n
        slab = lambda *g: jnp.minimum(step_of(*g), n - 1)
        in_specs.append(pl.BlockSpec((None, rb, width), lambda *g, l=layer, c=cb: (l, slab(*g), c)))
        out_specs.append(pl.BlockSpec((rb, width), lambda *g: (slab(*g), 0)))
        out_shape.append(jax.ShapeDtypeStruct((src.shape[1], width), BF16))
    return in_specs, out_specs, out_shape


def _with_side_casts(body, n_pre, n_in, n_cast):
    if n_cast == 0:
        return body

    def kern(*refs):
        head = n_pre + n_in
        slabs_in = refs[head:head + n_cast]
        slabs_out = refs[head + n_cast + 1:head + 2 * n_cast + 1]
        for src, dst in zip(slabs_in, slabs_out):
            dst[...] = src[...].astype(BF16)
        body(*refs[:head], refs[head + n_cast], *refs[head + 2 * n_cast + 1:])
    return kern


def _rmsnorm_kernel(x_ref, g_ref, o_ref):
    x = x_ref[...]
    ms = jnp.mean(x * x, axis=-1, keepdims=True)
    o_ref[...] = (x * lax.rsqrt(ms + EPS) * g_ref[...]).astype(o_ref.dtype)


def _rmsnorm(x, g, layer, casts, tr=512):
    t, d = x.shape
    steps = t // tr
    c_in, c_out, c_shape = _cast_specs(casts, steps, lambda i: i)
    outs = pl.pallas_call(
        _with_side_casts(_rmsnorm_kernel, 0, 2, len(casts)),
        out_shape=[jax.ShapeDtypeStruct((t, d), BF16)] + c_shape,
        grid=(steps,),
        in_specs=[pl.BlockSpec((tr, d), lambda i: (i, 0)),
                  pl.BlockSpec((None, 1, d), lambda i: (layer, 0, 0))] + c_in,
        out_specs=[pl.BlockSpec((tr, d), lambda i: (i, 0))] + c_out,
        compiler_params=_params(("arbitrary",)),
        name="rmsnorm",
    )(x, g.reshape(g.shape[0], 1, d), *[c[0] for c in casts])
    return outs[0], list(outs[1:])


def _plain_body(a_ref, w_ref, o_ref):
    for rows in _row_slabs(a_ref.shape[0], MXU_ROWS):
        acc = jnp.dot(a_ref[rows, :], w_ref[...], preferred_element_type=F32)
        o_ref[rows, :] = acc.astype(o_ref.dtype)


def _silu_body(a_ref, w_ref, o_ref):
    for rows in _row_slabs(a_ref.shape[0], EPILOGUE_ROWS):
        acc = jnp.dot(a_ref[rows, :], w_ref[...], preferred_element_type=F32)
        o_ref[rows, :] = _silu(acc).astype(o_ref.dtype)


def _rope_body(lg_ref, a_ref, w_ref, cos_ref, sin_ref, o_ref, *, chunk):
    j = pl.program_id(0)
    heads_per_tile = o_ref.shape[1] // HEAD_QK
    for rows in _row_slabs(a_ref.shape[0], EPILOGUE_ROWS):
        acc = jnp.dot(a_ref[rows, :], w_ref[...], preferred_element_type=F32)
        cos = cos_ref[rows, :]
        sin = sin_ref[rows, :]
        row = rows.start + lax.broadcasted_iota(jnp.int32, (rows.size, 1), 0)
        pos1 = ((row % chunk) + 1).astype(F32)
        for hh in range(heads_per_tile):
            slot = j * heads_per_tile + hh
            fac = jnp.exp(pos1 * lg_ref[slot])
            fac = fac * jnp.where(slot < N_RET_HEADS, 1.0, HEAD_QK ** -0.5).astype(F32)
            cf = cos * fac
            sf = sin * fac
            a = hh * HEAD_QK
            x1 = acc[:, a:a + ROPE_HALF]
            x2 = acc[:, a + ROPE_HALF:a + HEAD_QK]
            o_ref[rows, a:a + ROPE_HALF] = (x1 * cf - x2 * sf).astype(o_ref.dtype)
            o_ref[rows, a + ROPE_HALF:a + HEAD_QK] = (x1 * sf + x2 * cf).astype(o_ref.dtype)


def _ret_proj_kernel(lg_ref, a_ref, w_ref, cos_ref, sin_ref, o_ref, *, chunk, n_rope, n_plain):
    j = pl.program_id(0)

    @pl.when(j < n_rope)
    def _():
        _rope_body(lg_ref, a_ref, w_ref, cos_ref, sin_ref, o_ref, chunk=chunk)

    @pl.when(jnp.logical_and(j >= n_rope, j < n_rope + n_plain))
    def _():
        _plain_body(a_ref, w_ref, o_ref)

    @pl.when(j >= n_rope + n_plain)
    def _():
        _silu_body(a_ref, w_ref, o_ref)


def _pool_gate_body(a_ref, w_ref, u_ref, uh_ref, b_ref, sc_ref, o_ref, *, levels, blk, tm):
    u = u_ref[...]
    halo = jnp.where(blk == 0, 0.0, uh_ref[...])
    s = jnp.concatenate([halo, u], axis=0)
    for lvl in range(levels):
        s = s + pltpu.roll(s, 1 << lvl, 0)
    tpos = (blk * tm + lax.broadcasted_iota(jnp.int32, (tm, 1), 0)).astype(F32)
    inv_cnt = 1.0 / jnp.minimum(tpos + 1.0, float(1 << levels))
    y = s[POOL_HALO:, :] * inv_cnt - u
    z = jnp.dot(a_ref[...], w_ref[...], preferred_element_type=F32)
    o_ref[...] = ((y + b_ref[...]) * sc_ref[...] * _silu(z)).astype(o_ref.dtype)


def _pool_gate_kernel(*refs, tm, bps):
    g = pl.program_id(0)
    blk = pl.program_id(1) % bps
    for k, win in enumerate(POOL_WINDOWS):
        @pl.when(g == k)
        def _(levels=win.bit_length() - 1):
            _pool_gate_body(*refs, levels=levels, blk=blk, tm=tm)


def _proj_in(body, a, w, col0, ncols, out_dtype, *, extra=(), extra_specs=(),
             prefetch=None, casts=(), tm, tn=1024, name):
    t, kdim = a.shape
    jb = col0 // tn
    nb = t // tm
    grid = (ncols // tn, nb)
    c_in, c_out, c_shape = _cast_specs(casts, grid[0] * nb, lambda j, i, *_: j * nb + i)
    in_specs = [pl.BlockSpec((tm, kdim), lambda j, i, *_: (i, 0)),
                pl.BlockSpec((None, kdim, tn), lambda j, i, *_: (0, 0, jb + j))]
    in_specs += list(extra_specs) + c_in
    out_specs = [pl.BlockSpec((tm, tn), lambda j, i, *_: (i, j))] + c_out
    out_shape = [jax.ShapeDtypeStruct((t, ncols), out_dtype)] + c_shape
    n_pre = 0 if prefetch is None else 1
    kern = _with_side_casts(body, n_pre, 2 + len(extra), len(casts))
    gs = pltpu.PrefetchScalarGridSpec(num_scalar_prefetch=n_pre, grid=grid,
                                      in_specs=in_specs, out_specs=out_specs)
    args = ([] if prefetch is None else [prefetch]) + [a, w, *extra] + [c[0] for c in casts]
    outs = pl.pallas_call(kern, out_shape=out_shape, grid_spec=gs,
                          compiler_params=_params(("arbitrary", "arbitrary")), name=name)(*args)
    return outs[0], list(outs[1:])


def _proj_out_kernel(a_ref, w_ref, r_ref, g_ref, *o_refs, last):
    h = r_ref[...] + jnp.dot(a_ref[...], w_ref[...], preferred_element_type=F32)
    ms = jnp.mean(h * h, axis=-1, keepdims=True)
    hn = h * lax.rsqrt(ms + EPS) * g_ref[...]
    if last:
        o_refs[0][...] = hn
    else:
        o_refs[0][...] = h
        o_refs[1][...] = hn.astype(BF16)


def _proj_out(a, w, resid, g, g_layer, *, last, tm=512):
    t, kdim = a.shape
    n = w.shape[1]
    row = lambda i: (i, 0)
    in_specs = [pl.BlockSpec((tm, kdim), row),
                pl.BlockSpec((kdim, n), lambda i: (0, 0), pipeline_mode=pl.Buffered(1)),
                pl.BlockSpec((tm, n), row),
                pl.BlockSpec((None, 1, n), lambda i: (g_layer, 0, 0))]
    if last:
        out_shape = jax.ShapeDtypeStruct((t, n), F32)
        out_specs = pl.BlockSpec((tm, n), row)
    else:
        out_shape = (jax.ShapeDtypeStruct((t, n), F32), jax.ShapeDtypeStruct((t, n), BF16))
        out_specs = (pl.BlockSpec((tm, n), row), pl.BlockSpec((tm, n), row))
    return pl.pallas_call(
        functools.partial(_proj_out_kernel, last=last),
        out_shape=out_shape,
        grid=(t // tm,),
        in_specs=in_specs,
        out_specs=out_specs,
        compiler_params=_params(("arbitrary",)),
        name="proj_out",
    )(a, w, resid, g.reshape(g.shape[0], 1, n))


def _retention_kernel(lg_ref, q_ref, k_ref, v_ref, sz_ref, gn_ref, o_ref,
                      state_ref, p_ref, upd_ref, sb_ref, *, chunk, n_sub):
    h = pl.program_id(1)
    c = pl.program_id(2)

    @pl.when(c == 0)
    def _():
        state_ref[...] = jnp.zeros_like(state_ref)

    chunk_decay = jnp.exp(jnp.full((1, 1), float(chunk), F32) * lg_ref[h])
    ii = lax.broadcasted_iota(jnp.int32, (chunk, chunk), 0)
    jj = lax.broadcasted_iota(jnp.int32, (chunk, chunk), 1)
    causal = ii >= jj
    gn = gn_ref[...]

    for s in range(n_sub):
        rows = pl.ds(s * chunk, chunk)
        k = k_ref[rows, :]
        scores = lax.dot_general(q_ref[rows, :], k, (((1,), (1,)), ((), ())),
                                 preferred_element_type=F32)
        p_ref[s] = jnp.where(causal, scores, 0.0).astype(BF16)
        upd_ref[s] = lax.dot_general(k, v_ref[rows, :], (((0,), (0,)), ((), ())),
                                     preferred_element_type=F32)

    state = state_ref[...]
    for s in range(n_sub):
        sb_ref[s] = state.astype(BF16)
        state = chunk_decay * (state + upd_ref[s])
    state_ref[...] = state

    for s in range(n_sub):
        rows = pl.ds(s * chunk, chunk)
        lhs = jnp.concatenate([p_ref[s], q_ref[rows, :]], axis=1)
        rhs = jnp.concatenate([v_ref[rows, :], sb_ref[s]], axis=0)
        o = jnp.dot(lhs, rhs, preferred_element_type=F32)

        mu = jnp.mean(o, axis=-1, keepdims=True)
        oc = o - mu
        var = jnp.mean(oc * oc, axis=-1, keepdims=True)
        y = oc * lax.rsqrt(var + EPS) * gn
        o_ref[rows, :] = (y * sz_ref[rows, :].astype(F32)).astype(o_ref.dtype)


def _retention(p, lg, gn, layer, *, batch, seq, rows=4096):
    t = p.shape[0]
    nh = N_RET_HEADS
    rows = min(rows, seq)
    cps = seq // rows
    n_sub = rows // RET_CHUNK
    kern = functools.partial(_retention_kernel, chunk=RET_CHUNK, n_sub=n_sub)
    v_blk = 2 * nh * HEAD_QK // HEAD_V

    def blk(first):
        return lambda b, h, c: (b * cps + c, first + h)

    return pl.pallas_call(
        kern,
        out_shape=jax.ShapeDtypeStruct((t, nh * HEAD_V), BF16),
        grid=(batch, nh, cps),
        in_specs=[pl.BlockSpec(memory_space=pltpu.SMEM),
                  pl.BlockSpec((rows, HEAD_QK), blk(0)),
                  pl.BlockSpec((rows, HEAD_QK), blk(nh)),
                  pl.BlockSpec((rows, HEAD_V), blk(v_blk)),
                  pl.BlockSpec((rows, HEAD_V), blk(v_blk + nh)),
                  pl.BlockSpec((None, 1, HEAD_V), lambda b, h, c: (layer, 0, h))],
        out_specs=pl.BlockSpec((rows, HEAD_V), blk(0)),
        scratch_shapes=[pltpu.VMEM((HEAD_QK, HEAD_V), F32),
                        pltpu.VMEM((n_sub, RET_CHUNK, RET_CHUNK), BF16),
                        pltpu.VMEM((n_sub, HEAD_QK, HEAD_V), F32),
                        pltpu.VMEM((n_sub, HEAD_QK, HEAD_V), BF16)],
        compiler_params=_params(("arbitrary", "arbitrary", "arbitrary")),
        name="retention",
    )(lg, p, p, p, p, gn.reshape(gn.shape[0], 1, nh * HEAD_V))


def _rope_tables(seq):
    inv = ROPE_BASE ** (-jnp.arange(ROPE_HALF, dtype=F32) / ROPE_HALF)
    ang = jnp.arange(seq, dtype=F32)[:, None] * inv[None, :]
    return jnp.cos(ang), jnp.sin(ang)


def _retention_layer(hn, w_in, gn, layer, cos, sin, lg, casts, *, batch, seq, tm, tn=1024):
    qk_cols = 2 * N_RET_HEADS * HEAD_QK
    v_cols = N_RET_HEADS * HEAD_V
    bps = seq // tm
    kern = functools.partial(_ret_proj_kernel, chunk=RET_CHUNK,
                             n_rope=qk_cols // tn, n_plain=v_cols // tn)
    tab = pl.BlockSpec((tm, ROPE_HALF), lambda j, i, *_: (i % bps, 0))
    p, got = _proj_in(kern, hn, w_in, 0, qk_cols + 2 * v_cols, BF16, extra=(cos, sin),
                      extra_specs=(tab, tab), prefetch=jnp.concatenate([lg, -lg]),
                      casts=casts, tm=tm, tn=tn, name="proj_ret")
    return _retention(p, lg, gn, layer, batch=batch, seq=seq), got


def _fold_group_kernel(wu_ref, wg_ref, o_ref):
    o_ref[...] = jnp.dot(wu_ref[...].astype(BF16), wg_ref[...].astype(BF16),
                         preferred_element_type=F32).astype(o_ref.dtype)


def _fold_group_weights(w_in, w_grp, layer):
    _, ng, gdim, _ = w_grp.shape
    d = w_in.shape[1]
    return pl.pallas_call(
        _fold_group_kernel,
        out_shape=jax.ShapeDtypeStruct((1, d, ng * gdim), BF16),
        grid=(ng,),
        in_specs=[pl.BlockSpec((None, d, gdim), lambda g: (layer, 0, g)),
                  pl.BlockSpec((None, None, gdim, gdim), lambda g: (layer, g, 0, 0))],
        out_specs=pl.BlockSpec((None, d, gdim), lambda g: (0, 0, g)),
        compiler_params=_params(("arbitrary",)),
        name="fold_group",
    )(w_in, w_grp)


def _pooling_layer(hn, w_fold, w_z, b_grp, scale, layer, gate_casts, *, seq, tm):
    ng = len(POOL_WINDOWS)
    n_layers, width = b_grp.shape[0], w_z.shape[2]
    gdim = width // ng
    assert POOL_WINDOWS[-1] <= POOL_HALO
    u, _ = _proj_in(_plain_body, hn, w_fold, 0, width, F32, tm=tm, tn=gdim, name="proj_u")
    tg = tm // 2
    hb = tg // POOL_HALO
    row_vec = pl.BlockSpec((None, 1, gdim), lambda j, i: (layer, 0, j))
    extra_specs = (pl.BlockSpec((tg, gdim), lambda j, i: (i, j)),
                   pl.BlockSpec((POOL_HALO, gdim), lambda j, i: (jnp.maximum(i * hb - 1, 0), j)),
                   row_vec, row_vec)
    return _proj_in(functools.partial(_pool_gate_kernel, tm=tg, bps=seq // tg),
                    hn, w_z, 0, width, BF16,
                    extra=(u, u, b_grp.reshape(n_layers, 1, width), scale.reshape(n_layers, 1, width)),
                    extra_specs=extra_specs, casts=gate_casts, tm=tg, tn=gdim, name="pool_gate")


def kernel(x, ret_norm, ret_w_in, ret_gn, ret_w_out, pool_norm, pool_w_in, pool_w_grp, pool_b_grp, pool_scale, pool_w_out, final_norm):
    batch, seq, d = x.shape
    n_ret, n_pool = ret_norm.shape[0], pool_norm.shape[0]
    depth = n_ret + n_pool
    tm = 2048
    width = pool_w_grp.shape[1] * pool_w_grp.shape[2]
    cos, sin = _rope_tables(seq)
    lg = jnp.log1p(-jnp.exp2(-5.0 - jnp.arange(N_RET_HEADS, dtype=F32)))

    h = x.reshape(batch * seq, d)
    hn, (ret_in,) = _rmsnorm(h, ret_norm, 0, [_whole(ret_w_in, 0)])
    pool = None
    for i in range(depth):
        j = i // 2
        if i % 2 == 0:
            has_pool = j < n_pool
            casts = [_whole(ret_w_out, j)]
            if has_pool:
                casts += [(pool_w_in, j, 1, width), _whole(pool_w_out, j)]
            act, got = _retention_layer(hn, ret_in[None], ret_gn, j, cos, sin, lg, casts,
                                        batch=batch, seq=seq, tm=tm)
            w_out = got[0]
            pool = tuple(got[1:]) if has_pool else None
        else:
            has_ret = j + 1 < n_ret
            act, got = _pooling_layer(hn, _fold_group_weights(pool_w_in, pool_w_grp, j), pool[0][None],
                                      pool_b_grp, pool_scale, j,
                                      [_whole(ret_w_in, j + 1)] if has_ret else [], seq=seq, tm=tm)
            w_out = pool[1]
            ret_in = got[0] if has_ret else None
        if i + 1 == depth:
            out = _proj_out(act, w_out, h, final_norm.reshape(1, d), 0, last=True)
        else:
            nxt, nj = (pool_norm, j) if i % 2 == 0 else (ret_norm, j + 1)
            h, hn = _proj_out(act, w_out, h, nxt, nj, last=False)
    return out.reshape(batch, seq, d)
```

```python
import functools

import jax
import jax.numpy as jnp
from jax import lax
from jax.experimental import pallas as pl
from jax.experimental.pallas import tpu as pltpu

F32 = jnp.float32
BF16 = jnp.bfloat16

EPS = 1e-6
LOG2_E = 1.4426950408889634
N_RET_HEADS = 8
HEAD_QK = 256
HEAD_V = 512
ROPE_HALF = HEAD_QK // 2
ROPE_BASE = 10000.0
POOL_WINDOWS = (2, 4, 8, 16)
POOL_HALO = 16
RET_CHUNK = 256
MXU_ROWS = 512
EPILOGUE_ROWS = 256

VMEM_LIMIT = 56 * 1024 * 1024


def _params(sem):
    return pltpu.CompilerParams(dimension_semantics=sem, vmem_limit_bytes=VMEM_LIMIT)


def _silu(z):
    return z * (1.0 / (1.0 + jnp.exp2(z * (-LOG2_E))))


def _row_slabs(tm, slab):
    return [pl.ds(r, min(slab, tm)) for r in range(0, tm, slab)]


def _whole(src, layer):
    return (src, layer, 0, src.shape[2])


def _cast_specs(casts, steps, step_of):
    n = 1 << (steps.bit_length() - 1)
    in_specs, out_specs, out_shape = [], [], []
    for src, layer, cb, width in casts:
        rb = src.shape[1] // n
        slab = lambda *g: jnp.minimum(step_of(*g), n - 1)
        in_specs.append(pl.BlockSpec((None, rb, width), lambda *g, l=layer, c=cb: (l, slab(*g), c)))
        out_specs.append(pl.BlockSpec((rb, width), lambda *g: (slab(*g), 0)))
        out_shape.append(jax.ShapeDtypeStruct((src.shape[1], width), BF16))
    return in_specs, out_specs, out_shape


def _with_side_casts(body, n_pre, n_in, n_cast):
    if n_cast == 0:
        return body

    def kern(*refs):
        head = n_pre + n_in
        slabs_in = refs[head:head + n_cast]
        slabs_out = refs[head + n_cast + 1:head + 2 * n_cast + 1]
        for src, dst in zip(slabs_in, slabs_out):
            dst[...] = src[...].astype(BF16)
        body(*refs[:head], refs[head + n_cast], *refs[head + 2 * n_cast + 1:])
    return kern


def _rmsnorm_kernel(x_ref, g_ref, o_ref):
    x = x_ref[...]
    ms = jnp.mean(x * x, axis=-1, keepdims=True)
    o_ref[...] = (x * lax.rsqrt(ms + EPS) * g_ref[...]).astype(o_ref.dtype)


def _rmsnorm(x, g, layer, casts, tr=512):
    t, d = x.shape
    steps = t // tr
    c_in, c_out, c_shape = _cast_specs(casts, steps, lambda i: i)
    outs = pl.pallas_call(
        _with_side_casts(_rmsnorm_kernel, 0, 2, len(casts)),
        out_shape=[jax.ShapeDtypeStruct((t, d), BF16)] + c_shape,
        grid=(steps,),
        in_specs=[pl.BlockSpec((tr, d), lambda i: (i, 0)),
                  pl.BlockSpec((None, 1, d), lambda i: (layer, 0, 0))] + c_in,
        out_specs=[pl.BlockSpec((tr, d), lambda i: (i, 0))] + c_out,
        compiler_params=_params(("arbitrary",)),
        name="rmsnorm",
    )(x, g.reshape(g.shape[0], 1, d), *[c[0] for c in casts])
    return outs[0], list(outs[1:])


def _plain_body(a_ref, w_ref, o_ref):
    for rows in _row_slabs(a_ref.shape[0], MXU_ROWS):
        acc = jnp.dot(a_ref[rows, :], w_ref[...], preferred_element_type=F32)
        o_ref[rows, :] = acc.astype(o_ref.dtype)


def _silu_body(a_ref, w_ref, o_ref):
    for rows in _row_slabs(a_ref.shape[0], EPILOGUE_ROWS):
        acc = jnp.dot(a_ref[rows, :], w_ref[...], preferred_element_type=F32)
        o_ref[rows, :] = _silu(acc).astype(o_ref.dtype)


def _rope_body(lg_ref, a_ref, w_ref, cos_ref, sin_ref, o_ref, *, j, chunk):
    heads_per_tile = o_ref.shape[1] // HEAD_QK
    for rows in _row_slabs(a_ref.shape[0], EPILOGUE_ROWS):
        acc = jnp.dot(a_ref[rows, :], w_ref[...], preferred_element_type=F32)
        cos = cos_ref[rows, :]
        sin = sin_ref[rows, :]
        row = rows.start + lax.broadcasted_iota(jnp.int32, (rows.size, 1), 0)
        pos1 = ((row % chunk) + 1).astype(F32)
        for hh in range(heads_per_tile):
            slot = j * heads_per_tile + hh
            fac = jnp.exp(pos1 * lg_ref[slot])
            fac = fac * jnp.where(slot < N_RET_HEADS, 1.0, HEAD_QK ** -0.5).astype(F32)
            cf = cos * fac
            sf = sin * fac
            a = hh * HEAD_QK
            x1 = acc[:, a:a + ROPE_HALF]
            x2 = acc[:, a + ROPE_HALF:a + HEAD_QK]
            o_ref[rows, a:a + ROPE_HALF] = (x1 * cf - x2 * sf).astype(o_ref.dtype)
            o_ref[rows, a + ROPE_HALF:a + HEAD_QK] = (x1 * sf + x2 * cf).astype(o_ref.dtype)


def _ret_proj_kernel(lg_ref, a_ref, w_ref, cos_ref, sin_ref, o_ref, *, col_axis, chunk, n_rope, n_plain):
    j = pl.program_id(col_axis)

    @pl.when(j < n_rope)
    def _():
        _rope_body(lg_ref, a_ref, w_ref, cos_ref, sin_ref, o_ref, j=j, chunk=chunk)

    @pl.when(jnp.logical_and(j >= n_rope, j < n_rope + n_plain))
    def _():
        _plain_body(a_ref, w_ref, o_ref)

    @pl.when(j >= n_rope + n_plain)
    def _():
        _silu_body(a_ref, w_ref, o_ref)


def _pool_gate_body(a_ref, w_ref, u_ref, uh_ref, b_ref, sc_ref, o_ref, *, levels, blk, tm):
    u = u_ref[...]
    halo = jnp.where(blk == 0, 0.0, uh_ref[...])
    s = jnp.concatenate([halo, u], axis=0)
    for lvl in range(levels):
        s = s + pltpu.roll(s, 1 << lvl, 0)
    tpos = (blk * tm + lax.broadcasted_iota(jnp.int32, (tm, 1), 0)).astype(F32)
    inv_cnt = 1.0 / jnp.minimum(tpos + 1.0, float(1 << levels))
    y = s[POOL_HALO:, :] * inv_cnt - u
    z = jnp.dot(a_ref[...], w_ref[...], preferred_element_type=F32)
    o_ref[...] = ((y + b_ref[...] * sc_ref[...]) * _silu(z)).astype(o_ref.dtype)


def _pool_gate_kernel(*refs, tm, bps):
    g = pl.program_id(0)
    blk = pl.program_id(1) % bps
    for k, win in enumerate(POOL_WINDOWS):
        @pl.when(g == k)
        def _(levels=win.bit_length() - 1):
            _pool_gate_body(*refs, levels=levels, blk=blk, tm=tm)


def _proj_in(body, a, w, col0, ncols, out_dtype, *, extra=(), extra_specs=(),
             prefetch=None, casts=(), rows_outer=False, tm, tn=1024, name):
    t, kdim = a.shape
    jb = col0 // tn
    nb, nj = t // tm, ncols // tn
    if rows_outer:
        grid = (nb, nj)
        order = lambda f: (lambda i, j, *r: f(j, i, *r))
        step = lambda j, i: i * nj + j
    else:
        grid = (nj, nb)
        order = lambda f: f
        step = lambda j, i: j * nb + i
    spec = lambda shape, f: pl.BlockSpec(shape, order(f))
    c_in, c_out, c_shape = _cast_specs(casts, nb * nj, order(lambda j, i, *_: step(j, i)))
    in_specs = [spec((tm, kdim), lambda j, i, *_: (i, 0)),
                spec((None, kdim, tn), lambda j, i, *_: (0, 0, jb + j))]
    in_specs += [spec(shape, f) for shape, f in extra_specs] + c_in
    out_specs = [spec((tm, tn), lambda j, i, *_: (i, j))] + c_out
    out_shape = [jax.ShapeDtypeStruct((t, ncols), out_dtype)] + c_shape
    n_pre = 0 if prefetch is None else 1
    kern = _with_side_casts(body, n_pre, 2 + len(extra), len(casts))
    gs = pltpu.PrefetchScalarGridSpec(num_scalar_prefetch=n_pre, grid=grid,
                                      in_specs=in_specs, out_specs=out_specs)
    args = ([] if prefetch is None else [prefetch]) + [a, w, *extra] + [c[0] for c in casts]
    outs = pl.pallas_call(kern, out_shape=out_shape, grid_spec=gs,
                          compiler_params=_params(("arbitrary", "arbitrary")), name=name)(*args)
    return outs[0], list(outs[1:])


def _proj_out_kernel(a_ref, w_ref, r_ref, g_ref, *o_refs, last):
    h = r_ref[...] + jnp.dot(a_ref[...], w_ref[...], preferred_element_type=F32)
    ms = jnp.mean(h * h, axis=-1, keepdims=True)
    hn = h * lax.rsqrt(ms + EPS) * g_ref[...]
    if last:
        o_refs[0][...] = hn
    else:
        o_refs[0][...] = h
        o_refs[1][...] = hn.astype(BF16)


def _proj_out(a, w, resid, g, g_layer, *, last, tm=512):
    t, kdim = a.shape
    n = w.shape[1]
    row = lambda i: (i, 0)
    in_specs = [pl.BlockSpec((tm, kdim), row),
                pl.BlockSpec((kdim, n), lambda i: (0, 0), pipeline_mode=pl.Buffered(1)),
                pl.BlockSpec((tm, n), row),
                pl.BlockSpec((None, 1, n), lambda i: (g_layer, 0, 0))]
    if last:
        out_shape = jax.ShapeDtypeStruct((t, n), F32)
        out_specs = pl.BlockSpec((tm, n), row)
    else:
        out_shape = (jax.ShapeDtypeStruct((t, n), F32), jax.ShapeDtypeStruct((t, n), BF16))
        out_specs = (pl.BlockSpec((tm, n), row), pl.BlockSpec((tm, n), row))
    return pl.pallas_call(
        functools.partial(_proj_out_kernel, last=last),
        out_shape=out_shape,
        grid=(t // tm,),
        in_specs=in_specs,
        out_specs=out_specs,
        compiler_params=_params(("arbitrary",)),
        name="proj_out",
    )(a, w, resid, g.reshape(g.shape[0], 1, n))


def _retention_kernel(lg_ref, q_ref, k_ref, v_ref, sz_ref, gn_ref, o_ref,
                      state_ref, p_ref, upd_ref, sb_ref, *, chunk, n_sub):
    h = pl.program_id(1)
    c = pl.program_id(2)

    @pl.when(c == 0)
    def _():
        state_ref[...] = jnp.zeros_like(state_ref)

    chunk_decay = jnp.exp(jnp.full((1, 1), float(chunk), F32) * lg_ref[h])
    ii = lax.broadcasted_iota(jnp.int32, (chunk, chunk), 0)
    jj = lax.broadcasted_iota(jnp.int32, (chunk, chunk), 1)
    causal = ii >= jj
    gn = gn_ref[...]

    for s in range(n_sub):
        rows = pl.ds(s * chunk, chunk)
        k = k_ref[rows, :]
        scores = lax.dot_general(q_ref[rows, :], k, (((1,), (1,)), ((), ())),
                                 preferred_element_type=F32)
        p_ref[s] = jnp.where(causal, scores, 0.0).astype(BF16)
        upd_ref[s] = lax.dot_general(k, v_ref[rows, :], (((0,), (0,)), ((), ())),
                                     preferred_element_type=F32)

    state = state_ref[...]
    for s in range(n_sub):
        sb_ref[s] = state.astype(BF16)
        state = chunk_decay * (state + upd_ref[s])
    state_ref[...] = state

    for s in range(n_sub):
        rows = pl.ds(s * chunk, chunk)
        lhs = jnp.concatenate([p_ref[s], q_ref[rows, :]], axis=1)
        rhs = jnp.concatenate([v_ref[rows, :], sb_ref[s]], axis=0)
        o = jnp.dot(lhs, rhs, preferred_element_type=F32)

        mu = jnp.mean(o, axis=-1, keepdims=True)
        oc = o - mu
        var = jnp.mean(oc * oc, axis=-1, keepdims=True)
        y = oc * lax.rsqrt(var + EPS) * gn
        o_ref[rows, :] = (y * sz_ref[rows, :].astype(F32)).astype(o_ref.dtype)


def _retention(p, lg, gn, layer, *, batch, seq, rows=4096):
    t = p.shape[0]
    nh = N_RET_HEADS
    rows = min(rows, seq)
    cps = seq // rows
    n_sub = rows // RET_CHUNK
    kern = functools.partial(_retention_kernel, chunk=RET_CHUNK, n_sub=n_sub)
    v_blk = 2 * nh * HEAD_QK // HEAD_V

    def blk(first):
        return lambda b, h, c: (b * cps + c, first + h)

    return pl.pallas_call(
        kern,
        out_shape=jax.ShapeDtypeStruct((t, nh * HEAD_V), BF16),
        grid=(batch, nh, cps),
        in_specs=[pl.BlockSpec(memory_space=pltpu.SMEM),
                  pl.BlockSpec((rows, HEAD_QK), blk(0)),
                  pl.BlockSpec((rows, HEAD_QK), blk(nh)),
                  pl.BlockSpec((rows, HEAD_V), blk(v_blk)),
                  pl.BlockSpec((rows, HEAD_V), blk(v_blk + nh)),
                  pl.BlockSpec((None, 1, HEAD_V), lambda b, h, c: (layer, 0, h))],
        out_specs=pl.BlockSpec((rows, HEAD_V), blk(0)),
        scratch_shapes=[pltpu.VMEM((HEAD_QK, HEAD_V), F32),
                        pltpu.VMEM((n_sub, RET_CHUNK, RET_CHUNK), BF16),
                        pltpu.VMEM((n_sub, HEAD_QK, HEAD_V), F32),
                        pltpu.VMEM((n_sub, HEAD_QK, HEAD_V), BF16)],
        compiler_params=_params(("arbitrary", "arbitrary", "arbitrary")),
        name="retention",
    )(lg, p, p, p, p, gn.reshape(gn.shape[0], 1, nh * HEAD_V))


def _rope_tables(seq):
    inv = ROPE_BASE ** (-jnp.arange(ROPE_HALF, dtype=F32) / ROPE_HALF)
    ang = jnp.arange(seq, dtype=F32)[:, None] * inv[None, :]
    return jnp.cos(ang), jnp.sin(ang)


def _retention_layer(hn, w_in, gn, layer, cos, sin, lg, casts, *, batch, seq, tm, tn=1024):
    qk_cols = 2 * N_RET_HEADS * HEAD_QK
    v_cols = N_RET_HEADS * HEAD_V
    bps = seq // tm
    kern = functools.partial(_ret_proj_kernel, col_axis=1, chunk=RET_CHUNK,
                             n_rope=qk_cols // tn, n_plain=v_cols // tn)
    tab = ((tm, ROPE_HALF), lambda j, i, *_: (i % bps, 0))
    p, got = _proj_in(kern, hn, w_in, 0, qk_cols + 2 * v_cols, BF16, extra=(cos, sin),
                      extra_specs=(tab, tab), prefetch=jnp.concatenate([lg, -lg]),
                      casts=casts, rows_outer=True, tm=tm, tn=tn, name="proj_ret")
    return _retention(p, lg, gn, layer, batch=batch, seq=seq), got


def _fold_group_kernel(wu_ref, wg_ref, sc_ref, o_ref):
    prod = jnp.dot(wu_ref[...].astype(BF16), wg_ref[...].astype(BF16), preferred_element_type=F32)
    o_ref[...] = (prod * sc_ref[...]).astype(o_ref.dtype)


def _fold_group_weights(w_in, w_grp, scale, layer):
    n_layers, ng, gdim, _ = w_grp.shape
    d = w_in.shape[1]
    return pl.pallas_call(
        _fold_group_kernel,
        out_shape=jax.ShapeDtypeStruct((1, d, ng * gdim), BF16),
        grid=(ng,),
        in_specs=[pl.BlockSpec((None, d, gdim), lambda g: (layer, 0, g)),
                  pl.BlockSpec((None, None, gdim, gdim), lambda g: (layer, g, 0, 0)),
                  pl.BlockSpec((None, 1, gdim), lambda g: (layer, 0, g))],
        out_specs=pl.BlockSpec((None, d, gdim), lambda g: (0, 0, g)),
        compiler_params=_params(("arbitrary",)),
        name="fold_group",
    )(w_in, w_grp, scale.reshape(n_layers, 1, ng * gdim))


def _pooling_layer(hn, w_fold, w_z, b_grp, scale, layer, gate_casts, *, seq, tm):
    ng = len(POOL_WINDOWS)
    n_layers, width = b_grp.shape[0], w_z.shape[2]
    gdim = width // ng
    assert POOL_WINDOWS[-1] <= POOL_HALO
    u, _ = _proj_in(_plain_body, hn, w_fold, 0, width, F32, rows_outer=True,
                    tm=tm, tn=gdim, name="proj_u")
    tg = tm // 2
    hb = tg // POOL_HALO
    row_vec = ((None, 1, gdim), lambda j, i: (layer, 0, j))
    extra_specs = (((tg, gdim), lambda j, i: (i, j)),
                   ((POOL_HALO, gdim), lambda j, i: (jnp.maximum(i * hb - 1, 0), j)),
                   row_vec, row_vec)
    return _proj_in(functools.partial(_pool_gate_kernel, tm=tg, bps=seq // tg),
                    hn, w_z, 0, width, BF16,
                    extra=(u, u, b_grp.reshape(n_layers, 1, width), scale.reshape(n_layers, 1, width)),
                    extra_specs=extra_specs, casts=gate_casts, tm=tg, tn=gdim, name="pool_gate")


def kernel(x, ret_norm, ret_w_in, ret_gn, ret_w_out, pool_norm, pool_w_in, pool_w_grp, pool_b_grp, pool_scale, pool_w_out, final_norm):
    batch, seq, d = x.shape
    n_ret, n_pool = ret_norm.shape[0], pool_norm.shape[0]
    depth = n_ret + n_pool
    tm = 2048
    width = pool_w_grp.shape[1] * pool_w_grp.shape[2]
    cos, sin = _rope_tables(seq)
    lg = jnp.log1p(-jnp.exp2(-5.0 - jnp.arange(N_RET_HEADS, dtype=F32)))

    h = x.reshape(batch * seq, d)
    hn, (ret_in,) = _rmsnorm(h, ret_norm, 0, [_whole(ret_w_in, 0)])
    pool = None
    for i in range(depth):
        j = i // 2
        if i % 2 == 0:
            has_pool = j < n_pool
            casts = [_whole(ret_w_out, j)]
            if has_pool:
                casts += [(pool_w_in, j, 1, width), _whole(pool_w_out, j)]
            act, got = _retention_layer(hn, ret_in[None], ret_gn, j, cos, sin, lg, casts,
                                        batch=batch, seq=seq, tm=tm)
            w_out = got[0]
            pool = tuple(got[1:]) if has_pool else None
        else:
            has_ret = j + 1 < n_ret
            act, got = _pooling_layer(hn, _fold_group_weights(pool_w_in, pool_w_grp, pool_scale, j), pool[0][None],
                                      pool_b_grp, pool_scale, j,
                                      [_whole(ret_w_in, j + 1)] if has_ret else [], seq=seq, tm=tm)
            w_out = pool[1]
            ret_in = got[0] if has_ret else None
        if i + 1 == depth:
            out = _proj_out(act, w_out, h, final_norm.reshape(1, d), 0, last=True)
        else:
            nxt, nj = (pool_norm, j) if i % 2 == 0 else (ret_norm, j + 1)
            h, hn = _proj_out(act, w_out, h, nxt, nj, last=False)
    return out.reshape(batch, seq, d)
```

```python
import functools

import jax
import jax.numpy as jnp
from jax import lax
from jax.experimental import pallas as pl
from jax.experimental.pallas import tpu as pltpu

F32 = jnp.float32
BF16 = jnp.bfloat16

EPS = 1e-6
LOG2_E = 1.4426950408889634
N_RET_HEADS = 8
HEAD_QK = 256
HEAD_V = 512
ROPE_HALF = HEAD_QK // 2
ROPE_BASE = 10000.0
POOL_WINDOWS = (2, 4, 8, 16)
POOL_HALO = 16
RET_CHUNK = 256
MXU_ROWS = 512
EPILOGUE_ROWS = 256

VMEM_LIMIT = 56 * 1024 * 1024


def _params(sem):
    return pltpu.CompilerParams(dimension_semantics=sem, vmem_limit_bytes=VMEM_LIMIT)


def _silu(z):
    return z * (1.0 / (1.0 + jnp.exp2(z * (-LOG2_E))))


def _row_slabs(tm, slab):
    return [pl.ds(r, min(slab, tm)) for r in range(0, tm, slab)]


def _whole(src, layer, gain=None):
    return (src, layer, 0, src.shape[2], gain)


def _cast_specs(casts, steps, step_of):
    n = 1 << (steps.bit_length() - 1)
    slab = lambda *g: jnp.minimum(step_of(*g), n - 1)
    in_specs, out_specs, out_shape, args = [], [], [], []
    for src, layer, cb, width, gain in casts:
        rows = src.shape[1]
        rb = rows // n
        assert rb * n == rows and rb % 16 == 0, (rows, n)
        in_specs.append(pl.BlockSpec((None, rb, width), lambda *g, l=layer, c=cb: (l, slab(*g), c)))
        args.append(src)
        if gain is not None:
            in_specs.append(pl.BlockSpec((None, rb, 1), lambda *g, l=layer: (l, slab(*g), 0)))
            args.append(gain)
        out_specs.append(pl.BlockSpec((rb, width), lambda *g: (slab(*g), 0)))
        out_shape.append(jax.ShapeDtypeStruct((rows, width), BF16))
    return in_specs, out_specs, out_shape, args


def _with_side_casts(body, n_pre, n_in, casts):
    if not casts:
        return body
    has_gain = [c[4] is not None for c in casts]
    n_slab_in = len(casts) + sum(has_gain)

    def kern(*refs):
        head = n_pre + n_in
        slabs_in = list(refs[head:head + n_slab_in])
        out0 = head + n_slab_in
        slabs_out = refs[out0 + 1:out0 + 1 + len(casts)]
        for gained, dst in zip(has_gain, slabs_out):
            w = slabs_in.pop(0)[...]
            if gained:
                w = w * slabs_in.pop(0)[...]
            dst[...] = w.astype(BF16)
        body(*refs[:head], refs[out0], *refs[out0 + 1 + len(casts):])
    return kern


def _rmsnorm_kernel(x_ref, g_ref, o_ref):
    x = x_ref[...]
    ms = jnp.mean(x * x, axis=-1, keepdims=True)
    o_ref[...] = (x * lax.rsqrt(ms + EPS) * g_ref[...]).astype(o_ref.dtype)


def _rmsnorm(x, g, layer, casts, tr=512):
    t, d = x.shape
    steps = t // tr
    c_in, c_out, c_shape, c_args = _cast_specs(casts, steps, lambda i: i)
    outs = pl.pallas_call(
        _with_side_casts(_rmsnorm_kernel, 0, 2, casts),
        out_shape=[jax.ShapeDtypeStruct((t, d), BF16)] + c_shape,
        grid=(steps,),
        in_specs=[pl.BlockSpec((tr, d), lambda i: (i, 0)),
                  pl.BlockSpec((None, 1, d), lambda i: (layer, 0, 0))] + c_in,
        out_specs=[pl.BlockSpec((tr, d), lambda i: (i, 0))] + c_out,
        compiler_params=_params(("arbitrary",)),
        name="rmsnorm",
    )(x, g.reshape(g.shape[0], 1, d), *c_args)
    return outs[0], list(outs[1:])


def _plain_body(a_ref, w_ref, o_ref):
    for rows in _row_slabs(a_ref.shape[0], MXU_ROWS):
        acc = jnp.dot(a_ref[rows, :], w_ref[...], preferred_element_type=F32)
        o_ref[rows, :] = acc.astype(o_ref.dtype)


def _silu_body(a_ref, w_ref, o_ref):
    for rows in _row_slabs(a_ref.shape[0], EPILOGUE_ROWS):
        acc = jnp.dot(a_ref[rows, :], w_ref[...], preferred_element_type=F32)
        o_ref[rows, :] = _silu(acc).astype(o_ref.dtype)


def _rope_body(lg_ref, a_ref, w_ref, cos_ref, sin_ref, o_ref, *, j, chunk):
    heads_per_tile = o_ref.shape[1] // HEAD_QK
    for rows in _row_slabs(a_ref.shape[0], EPILOGUE_ROWS):
        acc = jnp.dot(a_ref[rows, :], w_ref[...], preferred_element_type=F32)
        cos = cos_ref[rows, :]
        sin = sin_ref[rows, :]
        row = rows.start + lax.broadcasted_iota(jnp.int32, (rows.size, 1), 0)
        pos1 = ((row % chunk) + 1).astype(F32)
        for hh in range(heads_per_tile):
            slot = j * heads_per_tile + hh
            fac = jnp.exp(pos1 * lg_ref[slot])
            fac = fac * jnp.where(slot < N_RET_HEADS, 1.0, HEAD_QK ** -0.5).astype(F32)
            cf = cos * fac
            sf = sin * fac
            a = hh * HEAD_QK
            x1 = acc[:, a:a + ROPE_HALF]
            x2 = acc[:, a + ROPE_HALF:a + HEAD_QK]
            o_ref[rows, a:a + ROPE_HALF] = (x1 * cf - x2 * sf).astype(o_ref.dtype)
            o_ref[rows, a + ROPE_HALF:a + HEAD_QK] = (x1 * sf + x2 * cf).astype(o_ref.dtype)


def _ret_proj_kernel(lg_ref, a_ref, w_ref, cos_ref, sin_ref, o_ref, *, col_axis, chunk, n_rope, n_plain):
    j = pl.program_id(col_axis)

    @pl.when(j < n_rope)
    def _():
        _rope_body(lg_ref, a_ref, w_ref, cos_ref, sin_ref, o_ref, j=j, chunk=chunk)

    @pl.when(jnp.logical_and(j >= n_rope, j < n_rope + n_plain))
    def _():
        _plain_body(a_ref, w_ref, o_ref)

    @pl.when(j >= n_rope + n_plain)
    def _():
        _silu_body(a_ref, w_ref, o_ref)


def _pool_gate_body(a_ref, w_ref, u_ref, uh_ref, b_ref, sc_ref, o_ref, *, levels, blk, tm):
    u = u_ref[...]
    halo = jnp.where(blk == 0, 0.0, uh_ref[...])
    s = jnp.concatenate([halo, u], axis=0)
    for lvl in range(levels):
        s = s + pltpu.roll(s, 1 << lvl, 0)
    tpos = (blk * tm + lax.broadcasted_iota(jnp.int32, (tm, 1), 0)).astype(F32)
    inv_cnt = 1.0 / jnp.minimum(tpos + 1.0, float(1 << levels))
    y = s[POOL_HALO:, :] * inv_cnt - u
    z = jnp.dot(a_ref[...], w_ref[...], preferred_element_type=F32)
    o_ref[...] = ((y + b_ref[...] * sc_ref[...]) * _silu(z)).astype(o_ref.dtype)


def _pool_gate_kernel(*refs, tm, bps):
    g = pl.program_id(0)
    blk = pl.program_id(1) % bps
    for k, win in enumerate(POOL_WINDOWS):
        @pl.when(g == k)
        def _(levels=win.bit_length() - 1):
            _pool_gate_body(*refs, levels=levels, blk=blk, tm=tm)


def _proj_in(body, a, w, col0, ncols, out_dtype, *, extra=(), extra_specs=(),
             prefetch=None, casts=(), rows_outer=False, tm, tn=1024, name):
    t, kdim = a.shape
    jb = col0 // tn
    nb, nj = t // tm, ncols // tn
    if rows_outer:
        grid = (nb, nj)
        order = lambda f: (lambda i, j, *r: f(j, i, *r))
        step = lambda j, i: i * nj + j
    else:
        grid = (nj, nb)
        order = lambda f: f
        step = lambda j, i: j * nb + i
    spec = lambda shape, f: pl.BlockSpec(shape, order(f))
    c_in, c_out, c_shape, c_args = _cast_specs(casts, nb * nj, order(lambda j, i, *_: step(j, i)))
    in_specs = [spec((tm, kdim), lambda j, i, *_: (i, 0)),
                spec((None, kdim, tn), lambda j, i, *_: (0, 0, jb + j))]
    in_specs += [spec(shape, f) for shape, f in extra_specs] + c_in
    out_specs = [spec((tm, tn), lambda j, i, *_: (i, j))] + c_out
    out_shape = [jax.ShapeDtypeStruct((t, ncols), out_dtype)] + c_shape
    n_pre = 0 if prefetch is None else 1
    kern = _with_side_casts(body, n_pre, 2 + len(extra), casts)
    gs = pltpu.PrefetchScalarGridSpec(num_scalar_prefetch=n_pre, grid=grid,
                                      in_specs=in_specs, out_specs=out_specs)
    args = ([] if prefetch is None else [prefetch]) + [a, w, *extra] + c_args
    outs = pl.pallas_call(kern, out_shape=out_shape, grid_spec=gs,
                          compiler_params=_params(("arbitrary", "arbitrary")), name=name)(*args)
    return outs[0], list(outs[1:])


def _proj_out_kernel(a_ref, w_ref, r_ref, g_ref, *o_refs, last):
    h = r_ref[...] + jnp.dot(a_ref[...], w_ref[...], preferred_element_type=F32)
    ms = jnp.mean(h * h, axis=-1, keepdims=True)
    hn = h * lax.rsqrt(ms + EPS) * g_ref[...]
    if last:
        o_refs[0][...] = hn
    else:
        o_refs[0][...] = h
        o_refs[1][...] = hn.astype(BF16)


def _proj_out(a, w, resid, g, g_layer, *, last, tm=512):
    t, kdim = a.shape
    n = w.shape[1]
    row = lambda i: (i, 0)
    in_specs = [pl.BlockSpec((tm, kdim), row),
                pl.BlockSpec((kdim, n), lambda i: (0, 0), pipeline_mode=pl.Buffered(1)),
                pl.BlockSpec((tm, n), row),
                pl.BlockSpec((None, 1, n), lambda i: (g_layer, 0, 0))]
    if last:
        out_shape = jax.ShapeDtypeStruct((t, n), F32)
        out_specs = pl.BlockSpec((tm, n), row)
    else:
        out_shape = (jax.ShapeDtypeStruct((t, n), F32), jax.ShapeDtypeStruct((t, n), BF16))
        out_specs = (pl.BlockSpec((tm, n), row), pl.BlockSpec((tm, n), row))
    return pl.pallas_call(
        functools.partial(_proj_out_kernel, last=last),
        out_shape=out_shape,
        grid=(t // tm,),
        in_specs=in_specs,
        out_specs=out_specs,
        compiler_params=_params(("arbitrary",)),
        name="proj_out",
    )(a, w, resid, g.reshape(g.shape[0], 1, n))


def _retention_kernel(lg_ref, q_ref, k_ref, v_ref, sz_ref, o_ref,
                      state_ref, p_ref, upd_ref, sb_ref, *, chunk, n_sub):
    h = pl.program_id(1)
    c = pl.program_id(2)

    @pl.when(c == 0)
    def _():
        state_ref[...] = jnp.zeros_like(state_ref)

    chunk_decay = jnp.exp(jnp.full((1, 1), float(chunk), F32) * lg_ref[h])
    ii = lax.broadcasted_iota(jnp.int32, (chunk, chunk), 0)
    jj = lax.broadcasted_iota(jnp.int32, (chunk, chunk), 1)
    causal = ii >= jj

    for s in range(n_sub):
        rows = pl.ds(s * chunk, chunk)
        k = k_ref[rows, :]
        scores = lax.dot_general(q_ref[rows, :], k, (((1,), (1,)), ((), ())),
                                 preferred_element_type=F32)
        p_ref[s] = jnp.where(causal, scores, 0.0).astype(BF16)
        upd_ref[s] = lax.dot_general(k, v_ref[rows, :], (((0,), (0,)), ((), ())),
                                     preferred_element_type=F32)

    state = state_ref[...]
    for s in range(n_sub):
        sb_ref[s] = state.astype(BF16)
        state = chunk_decay * (state + upd_ref[s])
    state_ref[...] = state

    for s in range(n_sub):
        rows = pl.ds(s * chunk, chunk)
        lhs = jnp.concatenate([p_ref[s], q_ref[rows, :]], axis=1)
        rhs = jnp.concatenate([v_ref[rows, :], sb_ref[s]], axis=0)
        o = jnp.dot(lhs, rhs, preferred_element_type=F32)

        mu = jnp.mean(o, axis=-1, keepdims=True)
        oc = o - mu
        var = jnp.mean(oc * oc, axis=-1, keepdims=True)
        y = oc * lax.rsqrt(var + EPS)
        o_ref[rows, :] = y.astype(BF16) * sz_ref[rows, :]


def _retention(p, lg, *, batch, seq, rows=4096):
    t = p.shape[0]
    nh = N_RET_HEADS
    rows = min(rows, seq)
    cps = seq // rows
    n_sub = rows // RET_CHUNK
    kern = functools.partial(_retention_kernel, chunk=RET_CHUNK, n_sub=n_sub)
    v_blk = 2 * nh * HEAD_QK // HEAD_V

    def blk(first):
        return lambda b, h, c: (b * cps + c, first + h)

    return pl.pallas_call(
        kern,
        out_shape=jax.ShapeDtypeStruct((t, nh * HEAD_V), BF16),
        grid=(batch, nh, cps),
        in_specs=[pl.BlockSpec(memory_space=pltpu.SMEM),
                  pl.BlockSpec((rows, HEAD_QK), blk(0)),
                  pl.BlockSpec((rows, HEAD_QK), blk(nh)),
                  pl.BlockSpec((rows, HEAD_V), blk(v_blk)),
                  pl.BlockSpec((rows, HEAD_V), blk(v_blk + nh))],
        out_specs=pl.BlockSpec((rows, HEAD_V), blk(0)),
        scratch_shapes=[pltpu.VMEM((HEAD_QK, HEAD_V), F32),
                        pltpu.VMEM((n_sub, RET_CHUNK, RET_CHUNK), BF16),
                        pltpu.VMEM((n_sub, HEAD_QK, HEAD_V), F32),
                        pltpu.VMEM((n_sub, HEAD_QK, HEAD_V), BF16)],
        compiler_params=_params(("arbitrary", "arbitrary", "arbitrary")),
        name="retention",
    )(lg, p, p, p, p)


def _rope_tables(seq):
    inv = ROPE_BASE ** (-jnp.arange(ROPE_HALF, dtype=F32) / ROPE_HALF)
    ang = jnp.arange(seq, dtype=F32)[:, None] * inv[None, :]
    return jnp.cos(ang), jnp.sin(ang)


def _retention_layer(hn, w_in, cos, sin, lg, casts, *, batch, seq, tm, tn=1024):
    qk_cols = 2 * N_RET_HEADS * HEAD_QK
    v_cols = N_RET_HEADS * HEAD_V
    bps = seq // tm
    kern = functools.partial(_ret_proj_kernel, col_axis=1, chunk=RET_CHUNK,
                             n_rope=qk_cols // tn, n_plain=v_cols // tn)
    tab = ((tm, ROPE_HALF), lambda j, i, *_: (i % bps, 0))
    p, got = _proj_in(kern, hn, w_in, 0, qk_cols + 2 * v_cols, BF16, extra=(cos, sin),
                      extra_specs=(tab, tab), prefetch=jnp.concatenate([lg, -lg]),
                      casts=casts, rows_outer=True, tm=tm, tn=tn, name="proj_ret")
    return _retention(p, lg, batch=batch, seq=seq), got


def _fold_group_kernel(wu_ref, wg_ref, sc_ref, o_ref):
    prod = jnp.dot(wu_ref[...].astype(BF16), wg_ref[...].astype(BF16), preferred_element_type=F32)
    o_ref[...] = (prod * sc_ref[...]).astype(o_ref.dtype)


def _fold_group_weights(w_in, w_grp, scale, layer):
    n_layers, ng, gdim, _ = w_grp.shape
    d = w_in.shape[1]
    return pl.pallas_call(
        _fold_group_kernel,
        out_shape=jax.ShapeDtypeStruct((1, d, ng * gdim), BF16),
        grid=(ng,),
        in_specs=[pl.BlockSpec((None, d, gdim), lambda g: (layer, 0, g)),
                  pl.BlockSpec((None, None, gdim, gdim), lambda g: (layer, g, 0, 0)),
                  pl.BlockSpec((None, 1, gdim), lambda g: (layer, 0, g))],
        out_specs=pl.BlockSpec((None, d, gdim), lambda g: (0, 0, g)),
        compiler_params=_params(("arbitrary",)),
        name="fold_group",
    )(w_in, w_grp, scale.reshape(n_layers, 1, ng * gdim))


def _pooling_layer(hn, w_fold, w_z, b_grp, scale, layer, gate_casts, *, seq, tm):
    ng = len(POOL_WINDOWS)
    n_layers, width = b_grp.shape[0], w_z.shape[2]
    gdim = width // ng
    assert POOL_WINDOWS[-1] <= POOL_HALO
    u, _ = _proj_in(_plain_body, hn, w_fold, 0, width, F32, rows_outer=True,
                    tm=tm, tn=gdim, name="proj_u")
    tg = tm // 2
    hb = tg // POOL_HALO
    row_vec = ((None, 1, gdim), lambda j, i: (layer, 0, j))
    extra_specs = (((tg, gdim), lambda j, i: (i, j)),
                   ((POOL_HALO, gdim), lambda j, i: (jnp.maximum(i * hb - 1, 0), j)),
                   row_vec, row_vec)
    return _proj_in(functools.partial(_pool_gate_kernel, tm=tg, bps=seq // tg),
                    hn, w_z, 0, width, BF16,
                    extra=(u, u, b_grp.reshape(n_layers, 1, width), scale.reshape(n_layers, 1, width)),
                    extra_specs=extra_specs, casts=gate_casts, tm=tg, tn=gdim, name="pool_gate")


def kernel(x, ret_norm, ret_w_in, ret_gn, ret_w_out, pool_norm, pool_w_in, pool_w_grp, pool_b_grp, pool_scale, pool_w_out, final_norm):
    batch, seq, d = x.shape
    n_ret, n_pool = ret_norm.shape[0], pool_norm.shape[0]
    depth = n_ret + n_pool
    tm = 2048
    width = pool_w_grp.shape[1] * pool_w_grp.shape[2]
    gn_rows = ret_gn.reshape(n_ret, ret_gn.shape[1], 1)
    cos, sin = _rope_tables(seq)
    lg = jnp.log1p(-jnp.exp2(-5.0 - jnp.arange(N_RET_HEADS, dtype=F32)))

    h = x.reshape(batch * seq, d)
    hn, (ret_in,) = _rmsnorm(h, ret_norm, 0, [_whole(ret_w_in, 0)])
    pool = None
    for i in range(depth):
        j = i // 2
        if i % 2 == 0:
            has_pool = j < n_pool
            casts = [_whole(ret_w_out, j, gain=gn_rows)]
            if has_pool:
                casts += [(pool_w_in, j, 1, width, None), _whole(pool_w_out, j)]
            act, got = _retention_layer(hn, ret_in[None], cos, sin, lg, casts,
                                        batch=batch, seq=seq, tm=tm)
            w_out = got[0]
            pool = tuple(got[1:]) if has_pool else None
        else:
            has_ret = j + 1 < n_ret
            act, got = _pooling_layer(hn, _fold_group_weights(pool_w_in, pool_w_grp, pool_scale, j), pool[0][None],
                                      pool_b_grp, pool_scale, j,
                                      [_whole(ret_w_in, j + 1)] if has_ret else [], seq=seq, tm=tm)
            w_out = pool[1]
            ret_in = got[0] if has_ret else None
        if i + 1 == depth:
            out = _proj_out(act, w_out, h, final_norm.reshape(1, d), 0, last=True)
        else:
            nxt, nj = (pool_norm, j) if i % 2 == 0 else (ret_norm, j + 1)
            h, hn = _proj_out(act, w_out, h, nxt, nj, last=False)
    return out.reshape(batch, seq, d)
```

```python
import functools

import jax
import jax.numpy as jnp
from jax import lax
from jax.experimental import pallas as pl
from jax.experimental.pallas import tpu as pltpu

F32 = jnp.float32
BF16 = jnp.bfloat16

EPS = 1e-6
LOG2_E = 1.4426950408889634
N_RET_HEADS = 8
HEAD_QK = 256
HEAD_V = 512
ROPE_HALF = HEAD_QK // 2
ROPE_BASE = 10000.0
POOL_WINDOWS = (2, 4, 8, 16)
POOL_HALO = 16
RET_CHUNK = 256
MXU_ROWS = 512
EPILOGUE_ROWS = 256

VMEM_LIMIT = 56 * 1024 * 1024


def _params(sem):
    return pltpu.CompilerParams(dimension_semantics=sem, vmem_limit_bytes=VMEM_LIMIT)


def _silu(z):
    return z * (1.0 / (1.0 + jnp.exp2(z * (-LOG2_E))))


def _row_slabs(tm, slab):
    return [pl.ds(r, min(slab, tm)) for r in range(0, tm, slab)]


def _whole(src, layer):
    return (src, layer, 0, src.shape[2])


def _cast_specs(casts, steps, step_of):
    n = 1 << (steps.bit_length() - 1)
    in_specs, out_specs, out_shape = [], [], []
    slab = lambda *g: jnp.minimum(step_of(*g), n - 1)
    for src, layer, cb, width in casts:
        rb = src.shape[1] // n
        assert rb * n == src.shape[1] and rb % 16 == 0, (src.shape, n)
        in_specs.append(pl.BlockSpec((None, rb, width), lambda *g, l=layer, c=cb: (l, slab(*g), c)))
        out_specs.append(pl.BlockSpec((rb, width), lambda *g: (slab(*g), 0)))
        out_shape.append(jax.ShapeDtypeStruct((src.shape[1], width), BF16))
    return in_specs, out_specs, out_shape


def _with_side_casts(body, n_pre, n_in, n_cast, n_out=1):
    if n_cast == 0:
        return body

    def kern(*refs):
        head = n_pre + n_in
        outs = head + n_cast
        slabs_in = refs[head:outs]
        slabs_out = refs[outs + n_out:outs + n_out + n_cast]
        for src, dst in zip(slabs_in, slabs_out):
            dst[...] = src[...].astype(BF16)
        body(*refs[:head], *refs[outs:outs + n_out], *refs[outs + n_out + n_cast:])
    return kern


def _fold_slab(wu_ref, wg_ref, sc_ref, o_ref):
    prod = jnp.dot(wu_ref[...].astype(BF16), wg_ref[...].astype(BF16), preferred_element_type=F32)
    o_ref[...] = (prod * sc_ref[...]).astype(o_ref.dtype)


def _rmsnorm_kernel(x_ref, g_ref, wu_ref, wg_ref, sc_ref, o_ref, wf_ref):
    x = x_ref[...]
    ms = jnp.mean(x * x, axis=-1, keepdims=True)
    o_ref[...] = (x * lax.rsqrt(ms + EPS) * g_ref[...]).astype(o_ref.dtype)
    _fold_slab(wu_ref, wg_ref, sc_ref, wf_ref)


def _rmsnorm(x, g, layer, w_in, w_grp, scale, casts, tr=512):
    t, d = x.shape
    n_layers, ng, gdim, _ = w_grp.shape
    steps = t // tr
    n_slab = steps // (n_layers * ng)
    rs = d // n_slab
    assert n_slab * n_layers * ng == steps and rs * n_slab == d and rs % 16 == 0
    pos = lambda i: (i // (ng * n_slab), (i // n_slab) % ng, i % n_slab)
    c_in, c_out, c_shape = _cast_specs(casts, steps, lambda i: i)
    outs = pl.pallas_call(
        _with_side_casts(_rmsnorm_kernel, 0, 5, len(casts), n_out=2),
        out_shape=[jax.ShapeDtypeStruct((t, d), BF16),
                   jax.ShapeDtypeStruct((n_layers, d, ng * gdim), BF16)] + c_shape,
        grid=(steps,),
        in_specs=[pl.BlockSpec((tr, d), lambda i: (i, 0)),
                  pl.BlockSpec((None, 1, d), lambda i: (layer, 0, 0)),
                  pl.BlockSpec((None, rs, gdim), lambda i: (pos(i)[0], pos(i)[2], pos(i)[1])),
                  pl.BlockSpec((None, None, gdim, gdim), lambda i: (pos(i)[0], pos(i)[1], 0, 0)),
                  pl.BlockSpec((None, 1, gdim), lambda i: (pos(i)[0], 0, pos(i)[1]))] + c_in,
        out_specs=[pl.BlockSpec((tr, d), lambda i: (i, 0)),
                   pl.BlockSpec((None, rs, gdim), lambda i: (pos(i)[0], pos(i)[2], pos(i)[1]))] + c_out,
        compiler_params=_params(("arbitrary",)),
        name="rmsnorm",
    )(x, g.reshape(g.shape[0], 1, d), w_in, w_grp, scale.reshape(n_layers, 1, ng * gdim),
      *[c[0] for c in casts])
    return outs[0], outs[1], list(outs[2:])


def _plain_body(a_ref, w_ref, o_ref):
    for rows in _row_slabs(a_ref.shape[0], MXU_ROWS):
        acc = jnp.dot(a_ref[rows, :], w_ref[...], preferred_element_type=F32)
        o_ref[rows, :] = acc.astype(o_ref.dtype)


def _silu_body(a_ref, w_ref, o_ref):
    for rows in _row_slabs(a_ref.shape[0], EPILOGUE_ROWS):
        acc = jnp.dot(a_ref[rows, :], w_ref[...], preferred_element_type=F32)
        o_ref[rows, :] = _silu(acc).astype(o_ref.dtype)


def _rope_body(lg_ref, a_ref, w_ref, cos_ref, sin_ref, o_ref, *, j, chunk):
    heads_per_tile = o_ref.shape[1] // HEAD_QK
    for rows in _row_slabs(a_ref.shape[0], EPILOGUE_ROWS):
        acc = jnp.dot(a_ref[rows, :], w_ref[...], preferred_element_type=F32)
        cos = cos_ref[rows, :]
        sin = sin_ref[rows, :]
        row = rows.start + lax.broadcasted_iota(jnp.int32, (rows.size, 1), 0)
        pos1 = ((row % chunk) + 1).astype(F32)
        for hh in range(heads_per_tile):
            slot = j * heads_per_tile + hh
            fac = jnp.exp(pos1 * lg_ref[slot])
            fac = fac * jnp.where(slot < N_RET_HEADS, 1.0, HEAD_QK ** -0.5).astype(F32)
            cf = cos * fac
            sf = sin * fac
            a = hh * HEAD_QK
            x1 = acc[:, a:a + ROPE_HALF]
            x2 = acc[:, a + ROPE_HALF:a + HEAD_QK]
            o_ref[rows, a:a + ROPE_HALF] = (x1 * cf - x2 * sf).astype(o_ref.dtype)
            o_ref[rows, a + ROPE_HALF:a + HEAD_QK] = (x1 * sf + x2 * cf).astype(o_ref.dtype)


def _ret_proj_kernel(lg_ref, a_ref, w_ref, cos_ref, sin_ref, o_ref, *, col_axis, chunk, n_rope, n_plain):
    j = pl.program_id(col_axis)

    @pl.when(j < n_rope)
    def _():
        _rope_body(lg_ref, a_ref, w_ref, cos_ref, sin_ref, o_ref, j=j, chunk=chunk)

    @pl.when(jnp.logical_and(j >= n_rope, j < n_rope + n_plain))
    def _():
        _plain_body(a_ref, w_ref, o_ref)

    @pl.when(j >= n_rope + n_plain)
    def _():
        _silu_body(a_ref, w_ref, o_ref)


def _pool_gate_body(a_ref, w_ref, u_ref, uh_ref, b_ref, sc_ref, o_ref, *, levels, blk, tm):
    u = u_ref[...]
    halo = jnp.where(blk == 0, 0.0, uh_ref[...])
    s = jnp.concatenate([halo, u], axis=0)
    for lvl in range(levels):
        s = s + pltpu.roll(s, 1 << lvl, 0)
    tpos = (blk * tm + lax.broadcasted_iota(jnp.int32, (tm, 1), 0)).astype(F32)
    inv_cnt = 1.0 / jnp.minimum(tpos + 1.0, float(1 << levels))
    y = s[POOL_HALO:, :] * inv_cnt - u
    z = jnp.dot(a_ref[...], w_ref[...], preferred_element_type=F32)
    o_ref[...] = ((y + b_ref[...] * sc_ref[...]) * _silu(z)).astype(o_ref.dtype)


def _pool_gate_kernel(*refs, tm, bps):
    g = pl.program_id(0)
    blk = pl.program_id(1) % bps
    for k, win in enumerate(POOL_WINDOWS):
        @pl.when(g == k)
        def _(levels=win.bit_length() - 1):
            _pool_gate_body(*refs, levels=levels, blk=blk, tm=tm)


def _proj_in(body, a, w, col0, ncols, out_dtype, *, extra=(), extra_specs=(),
             prefetch=None, casts=(), w_layer=0, rows_outer=False, tm, tn=1024, name):
    t, kdim = a.shape
    jb = col0 // tn
    nb, nj = t // tm, ncols // tn
    if rows_outer:
        grid = (nb, nj)
        order = lambda f: (lambda i, j, *r: f(j, i, *r))
        step = lambda j, i: i * nj + j
    else:
        grid = (nj, nb)
        order = lambda f: f
        step = lambda j, i: j * nb + i
    spec = lambda shape, f: pl.BlockSpec(shape, order(f))
    c_in, c_out, c_shape = _cast_specs(casts, nb * nj, order(lambda j, i, *_: step(j, i)))
    in_specs = [spec((tm, kdim), lambda j, i, *_: (i, 0)),
                spec((None, kdim, tn), lambda j, i, *_: (w_layer, 0, jb + j))]
    in_specs += [spec(shape, f) for shape, f in extra_specs] + c_in
    out_specs = [spec((tm, tn), lambda j, i, *_: (i, j))] + c_out
    out_shape = [jax.ShapeDtypeStruct((t, ncols), out_dtype)] + c_shape
    n_pre = 0 if prefetch is None else 1
    kern = _with_side_casts(body, n_pre, 2 + len(extra), len(casts))
    gs = pltpu.PrefetchScalarGridSpec(num_scalar_prefetch=n_pre, grid=grid,
                                      in_specs=in_specs, out_specs=out_specs)
    args = ([] if prefetch is None else [prefetch]) + [a, w, *extra] + [c[0] for c in casts]
    outs = pl.pallas_call(kern, out_shape=out_shape, grid_spec=gs,
                          compiler_params=_params(("arbitrary", "arbitrary")), name=name)(*args)
    return outs[0], list(outs[1:])


def _proj_out_kernel(a_ref, w_ref, r_ref, g_ref, *o_refs, last):
    h = r_ref[...] + jnp.dot(a_ref[...], w_ref[...], preferred_element_type=F32)
    ms = jnp.mean(h * h, axis=-1, keepdims=True)
    hn = h * lax.rsqrt(ms + EPS) * g_ref[...]
    if last:
        o_refs[0][...] = hn
    else:
        o_refs[0][...] = h
        o_refs[1][...] = hn.astype(BF16)


def _proj_out(a, w, resid, g, g_layer, *, last, tm=512):
    t, kdim = a.shape
    n = w.shape[1]
    row = lambda i: (i, 0)
    in_specs = [pl.BlockSpec((tm, kdim), row),
                pl.BlockSpec((kdim, n), lambda i: (0, 0), pipeline_mode=pl.Buffered(1)),
                pl.BlockSpec((tm, n), row),
                pl.BlockSpec((None, 1, n), lambda i: (g_layer, 0, 0))]
    if last:
        out_shape = jax.ShapeDtypeStruct((t, n), F32)
        out_specs = pl.BlockSpec((tm, n), row)
    else:
        out_shape = (jax.ShapeDtypeStruct((t, n), F32), jax.ShapeDtypeStruct((t, n), BF16))
        out_specs = (pl.BlockSpec((tm, n), row), pl.BlockSpec((tm, n), row))
    return pl.pallas_call(
        functools.partial(_proj_out_kernel, last=last),
        out_shape=out_shape,
        grid=(t // tm,),
        in_specs=in_specs,
        out_specs=out_specs,
        compiler_params=_params(("arbitrary",)),
        name="proj_out",
    )(a, w, resid, g.reshape(g.shape[0], 1, n))


def _retention_kernel(lg_ref, q_ref, k_ref, v_ref, sz_ref, gn_ref, o_ref,
                      state_ref, p_ref, upd_ref, sb_ref, *, chunk, n_sub):
    h = pl.program_id(1)
    c = pl.program_id(2)

    @pl.when(c == 0)
    def _():
        state_ref[...] = jnp.zeros_like(state_ref)

    chunk_decay = jnp.exp(jnp.full((1, 1), float(chunk), F32) * lg_ref[h])
    ii = lax.broadcasted_iota(jnp.int32, (chunk, chunk), 0)
    jj = lax.broadcasted_iota(jnp.int32, (chunk, chunk), 1)
    causal = ii >= jj
    gn = gn_ref[...]

    for s in range(n_sub):
        rows = pl.ds(s * chunk, chunk)
        k = k_ref[rows, :]
        scores = lax.dot_general(q_ref[rows, :], k, (((1,), (1,)), ((), ())),
                                 preferred_element_type=F32)
        p_ref[s] = jnp.where(causal, scores, 0.0).astype(BF16)
        upd_ref[s] = lax.dot_general(k, v_ref[rows, :], (((0,), (0,)), ((), ())),
                                     preferred_element_type=F32)

    state = state_ref[...]
    for s in range(n_sub):
        sb_ref[s] = state.astype(BF16)
        state = chunk_decay * (state + upd_ref[s])
    state_ref[...] = state

    for s in range(n_sub):
        rows = pl.ds(s * chunk, chunk)
        lhs = jnp.concatenate([p_ref[s], q_ref[rows, :]], axis=1)
        rhs = jnp.concatenate([v_ref[rows, :], sb_ref[s]], axis=0)
        o = jnp.dot(lhs, rhs, preferred_element_type=F32)

        mu = jnp.mean(o, axis=-1, keepdims=True)
        oc = o - mu
        var = jnp.mean(oc * oc, axis=-1, keepdims=True)
        y = oc * lax.rsqrt(var + EPS) * gn
        o_ref[rows, :] = (y * sz_ref[rows, :].astype(F32)).astype(o_ref.dtype)


def _retention(p, lg, gn, layer, *, batch, seq, rows=4096):
    t = p.shape[0]
    nh = N_RET_HEADS
    rows = min(rows, seq)
    cps = seq // rows
    n_sub = rows // RET_CHUNK
    kern = functools.partial(_retention_kernel, chunk=RET_CHUNK, n_sub=n_sub)
    v_blk = 2 * nh * HEAD_QK // HEAD_V

    def blk(first):
        return lambda b, h, c: (b * cps + c, first + h)

    return pl.pallas_call(
        kern,
        out_shape=jax.ShapeDtypeStruct((t, nh * HEAD_V), BF16),
        grid=(batch, nh, cps),
        in_specs=[pl.BlockSpec(memory_space=pltpu.SMEM),
                  pl.BlockSpec((rows, HEAD_QK), blk(0)),
                  pl.BlockSpec((rows, HEAD_QK), blk(nh)),
                  pl.BlockSpec((rows, HEAD_V), blk(v_blk)),
                  pl.BlockSpec((rows, HEAD_V), blk(v_blk + nh)),
                  pl.BlockSpec((None, 1, HEAD_V), lambda b, h, c: (layer, 0, h))],
        out_specs=pl.BlockSpec((rows, HEAD_V), blk(0)),
        scratch_shapes=[pltpu.VMEM((HEAD_QK, HEAD_V), F32),
                        pltpu.VMEM((n_sub, RET_CHUNK, RET_CHUNK), BF16),
                        pltpu.VMEM((n_sub, HEAD_QK, HEAD_V), F32),
                        pltpu.VMEM((n_sub, HEAD_QK, HEAD_V), BF16)],
        compiler_params=_params(("arbitrary", "arbitrary", "arbitrary")),
        name="retention",
    )(lg, p, p, p, p, gn.reshape(gn.shape[0], 1, nh * HEAD_V))


def _rope_tables(seq):
    inv = ROPE_BASE ** (-jnp.arange(ROPE_HALF, dtype=F32) / ROPE_HALF)
    ang = jnp.arange(seq, dtype=F32)[:, None] * inv[None, :]
    return jnp.cos(ang), jnp.sin(ang)


def _retention_layer(hn, w_in, gn, layer, cos, sin, lg, casts, *, batch, seq, tm, tn=1024):
    qk_cols = 2 * N_RET_HEADS * HEAD_QK
    v_cols = N_RET_HEADS * HEAD_V
    bps = seq // tm
    kern = functools.partial(_ret_proj_kernel, col_axis=1, chunk=RET_CHUNK,
                             n_rope=qk_cols // tn, n_plain=v_cols // tn)
    tab = ((tm, ROPE_HALF), lambda j, i, *_: (i % bps, 0))
    p, got = _proj_in(kern, hn, w_in, 0, qk_cols + 2 * v_cols, BF16, extra=(cos, sin),
                      extra_specs=(tab, tab), prefetch=jnp.concatenate([lg, -lg]),
                      casts=casts, rows_outer=True, tm=tm, tn=tn, name="proj_ret")
    return _retention(p, lg, gn, layer, batch=batch, seq=seq), got


def _pooling_layer(hn, w_fold, fold_layer, w_z, b_grp, scale, layer, gate_casts, *, seq, tm):
    ng = len(POOL_WINDOWS)
    n_layers, width = b_grp.shape[0], w_z.shape[2]
    gdim = width // ng
    assert POOL_WINDOWS[-1] <= POOL_HALO
    u, _ = _proj_in(_plain_body, hn, w_fold, 0, width, F32, w_layer=fold_layer, rows_outer=True,
                    tm=tm, tn=gdim, name="proj_u")
    tg = tm // 2
    hb = tg // POOL_HALO
    row_vec = ((None, 1, gdim), lambda j, i: (layer, 0, j))
    extra_specs = (((tg, gdim), lambda j, i: (i, j)),
                   ((POOL_HALO, gdim), lambda j, i: (jnp.maximum(i * hb - 1, 0), j)),
                   row_vec, row_vec)
    return _proj_in(functools.partial(_pool_gate_kernel, tm=tg, bps=seq // tg),
                    hn, w_z, 0, width, BF16,
                    extra=(u, u, b_grp.reshape(n_layers, 1, width), scale.reshape(n_layers, 1, width)),
                    extra_specs=extra_specs, casts=gate_casts, tm=tg, tn=gdim, name="pool_gate")


def kernel(x, ret_norm, ret_w_in, ret_gn, ret_w_out, pool_norm, pool_w_in, pool_w_grp, pool_b_grp, pool_scale, pool_w_out, final_norm):
    batch, seq, d = x.shape
    n_ret, n_pool = ret_norm.shape[0], pool_norm.shape[0]
    depth = n_ret + n_pool
    tm = 2048
    width = pool_w_grp.shape[1] * pool_w_grp.shape[2]
    cos, sin = _rope_tables(seq)
    lg = jnp.log1p(-jnp.exp2(-5.0 - jnp.arange(N_RET_HEADS, dtype=F32)))

    h = x.reshape(batch * seq, d)
    hn, w_fold, (ret_in,) = _rmsnorm(h, ret_norm, 0, pool_w_in, pool_w_grp, pool_scale,
                                     [_whole(ret_w_in, 0)])
    pool = None
    for i in range(depth):
        j = i // 2
        if i % 2 == 0:
            has_pool = j < n_pool
            casts = [_whole(ret_w_out, j)]
            if has_pool:
                casts += [(pool_w_in, j, 1, width), _whole(pool_w_out, j)]
            act, got = _retention_layer(hn, ret_in[None], ret_gn, j, cos, sin, lg, casts,
                                        batch=batch, seq=seq, tm=tm)
            w_out = got[0]
            pool = tuple(got[1:]) if has_pool else None
        else:
            has_ret = j + 1 < n_ret
            act, got = _pooling_layer(hn, w_fold, j, pool[0][None], pool_b_grp, pool_scale, j,
                                      [_whole(ret_w_in, j + 1)] if has_ret else [], seq=seq, tm=tm)
            w_out = pool[1]
            ret_in = got[0] if has_ret else None
        if i + 1 == depth:
            out = _proj_out(act, w_out, h, final_norm.reshape(1, d), 0, last=True)
        else:
            nxt, nj = (pool_norm, j) if i % 2 == 0 else (ret_norm, j + 1)
            h, hn = _proj_out(act, w_out, h, nxt, nj, last=False)
    return out.reshape(batch, seq, d)
```

```python
import functools

import jax
import jax.numpy as jnp
from jax import lax
from jax.experimental import pallas as pl
from jax.experimental.pallas import tpu as pltpu

F32 = jnp.float32
BF16 = jnp.bfloat16

EPS = 1e-6
LOG2_E = 1.4426950408889634
N_RET_HEADS = 8
HEAD_QK = 256
HEAD_V = 512
ROPE_HALF = HEAD_QK // 2
ROPE_BASE = 10000.0
POOL_WINDOWS = (2, 4, 8, 16)
POOL_HALO = 16
RET_CHUNK = 256
MXU_ROWS = 512
EPILOGUE_ROWS = 256

VMEM_LIMIT = 56 * 1024 * 1024


def _params(sem):
    return pltpu.CompilerParams(dimension_semantics=sem, vmem_limit_bytes=VMEM_LIMIT)


def _silu(z):
    return z * (1.0 / (1.0 + jnp.exp2(z * (-LOG2_E))))


def _row_slabs(tm, slab):
    return [pl.ds(r, min(slab, tm)) for r in range(0, tm, slab)]


def _whole(src, layer):
    return (src, layer, 0, src.shape[2])


def _cast_specs(casts, steps, step_of):
    n = 1 << (steps.bit_length() - 1)
    in_specs, out_specs, out_shape = [], [], []
    slab = lambda *g: jnp.minimum(step_of(*g), n - 1)
    for src, layer, cb, width in casts:
        rb = src.shape[1] // n
        assert rb * n == src.shape[1] and rb % 16 == 0, (src.shape, n)
        in_specs.append(pl.BlockSpec((None, rb, width), lambda *g, l=layer, c=cb: (l, slab(*g), c)))
        out_specs.append(pl.BlockSpec((rb, width), lambda *g: (slab(*g), 0)))
        out_shape.append(jax.ShapeDtypeStruct((src.shape[1], width), BF16))
    return in_specs, out_specs, out_shape


def _with_side_casts(body, n_pre, n_in, n_cast, n_out=1):
    if n_cast == 0:
        return body

    def kern(*refs):
        head = n_pre + n_in
        outs = head + n_cast
        slabs_in = refs[head:outs]
        slabs_out = refs[outs + n_out:outs + n_out + n_cast]
        for src, dst in zip(slabs_in, slabs_out):
            dst[...] = src[...].astype(BF16)
        body(*refs[:head], *refs[outs:outs + n_out], *refs[outs + n_out + n_cast:])
    return kern


def _fold_slab(wu_ref, wg_ref, sc_ref, o_ref):
    prod = jnp.dot(wu_ref[...].astype(BF16), wg_ref[...].astype(BF16), preferred_element_type=F32)
    o_ref[...] = (prod * sc_ref[...]).astype(o_ref.dtype)


def _rmsnorm_kernel(x_ref, g_ref, wu_ref, wg_ref, sc_ref, o_ref, wf_ref):
    x = x_ref[...]
    ms = jnp.mean(x * x, axis=-1, keepdims=True)
    o_ref[...] = (x * lax.rsqrt(ms + EPS) * g_ref[...]).astype(o_ref.dtype)
    _fold_slab(wu_ref, wg_ref, sc_ref, wf_ref)


def _rmsnorm(x, g, layer, w_in, w_grp, scale, casts, tr=512):
    t, d = x.shape
    n_layers, ng, gdim, _ = w_grp.shape
    steps = t // tr
    n_slab = steps // (n_layers * ng)
    rs = d // n_slab
    assert n_slab * n_layers * ng == steps and rs * n_slab == d and rs % 16 == 0
    pos = lambda i: (i // (ng * n_slab), (i // n_slab) % ng, i % n_slab)
    c_in, c_out, c_shape = _cast_specs(casts, steps, lambda i: i)
    outs = pl.pallas_call(
        _with_side_casts(_rmsnorm_kernel, 0, 5, len(casts), n_out=2),
        out_shape=[jax.ShapeDtypeStruct((t, d), BF16),
                   jax.ShapeDtypeStruct((n_layers, d, ng * gdim), BF16)] + c_shape,
        grid=(steps,),
        in_specs=[pl.BlockSpec((tr, d), lambda i: (i, 0)),
                  pl.BlockSpec((None, 1, d), lambda i: (layer, 0, 0)),
                  pl.BlockSpec((None, rs, gdim), lambda i: (pos(i)[0], pos(i)[2], pos(i)[1])),
                  pl.BlockSpec((None, None, gdim, gdim), lambda i: (pos(i)[0], pos(i)[1], 0, 0)),
                  pl.BlockSpec((None, 1, gdim), lambda i: (pos(i)[0], 0, pos(i)[1]))] + c_in,
        out_specs=[pl.BlockSpec((tr, d), lambda i: (i, 0)),
                   pl.BlockSpec((None, rs, gdim), lambda i: (pos(i)[0], pos(i)[2], pos(i)[1]))] + c_out,
        compiler_params=_params(("arbitrary",)),
        name="rmsnorm",
    )(x, g.reshape(g.shape[0], 1, d), w_in, w_grp, scale.reshape(n_layers, 1, ng * gdim),
      *[c[0] for c in casts])
    return outs[0], outs[1], list(outs[2:])


def _plain_body(a_ref, w_ref, o_ref):
    for rows in _row_slabs(a_ref.shape[0], MXU_ROWS):
        acc = jnp.dot(a_ref[rows, :], w_ref[...], preferred_element_type=F32)
        o_ref[rows, :] = acc.astype(o_ref.dtype)


def _silu_body(a_ref, w_ref, o_ref):
    for rows in _row_slabs(a_ref.shape[0], EPILOGUE_ROWS):
        acc = jnp.dot(a_ref[rows, :], w_ref[...], preferred_element_type=F32)
        o_ref[rows, :] = _silu(acc).astype(o_ref.dtype)


def _rope_body(lg_ref, a_ref, w_ref, cos_ref, sin_ref, o_ref, *, j, chunk):
    heads_per_tile = o_ref.shape[1] // HEAD_QK
    for rows in _row_slabs(a_ref.shape[0], EPILOGUE_ROWS):
        acc = jnp.dot(a_ref[rows, :], w_ref[...], preferred_element_type=F32)
        cos = cos_ref[rows, :]
        sin = sin_ref[rows, :]
        row = rows.start + lax.broadcasted_iota(jnp.int32, (rows.size, 1), 0)
        pos1 = ((row % chunk) + 1).astype(F32)
        for hh in range(heads_per_tile):
            slot = j * heads_per_tile + hh
            fac = jnp.exp(pos1 * lg_ref[slot])
            fac = fac * jnp.where(slot < N_RET_HEADS, 1.0, HEAD_QK ** -0.5).astype(F32)
            cf = cos * fac
            sf = sin * fac
            a = hh * HEAD_QK
            x1 = acc[:, a:a + ROPE_HALF]
            x2 = acc[:, a + ROPE_HALF:a + HEAD_QK]
            o_ref[rows, a:a + ROPE_HALF] = (x1 * cf - x2 * sf).astype(o_ref.dtype)
            o_ref[rows, a + ROPE_HALF:a + HEAD_QK] = (x1 * sf + x2 * cf).astype(o_ref.dtype)


def _ret_proj_kernel(lg_ref, a_ref, w_ref, cos_ref, sin_ref, o_ref, *, col_axis, chunk, n_rope, n_plain):
    j = pl.program_id(col_axis)

    @pl.when(j < n_rope)
    def _():
        _rope_body(lg_ref, a_ref, w_ref, cos_ref, sin_ref, o_ref, j=j, chunk=chunk)

    @pl.when(jnp.logical_and(j >= n_rope, j < n_rope + n_plain))
    def _():
        _plain_body(a_ref, w_ref, o_ref)

    @pl.when(j >= n_rope + n_plain)
    def _():
        _silu_body(a_ref, w_ref, o_ref)


def _pool_gate_body(a_ref, w_ref, u_ref, uh_ref, b_ref, sc_ref, o_ref, *, levels, blk, tm):
    u = u_ref[...]
    halo = jnp.where(blk == 0, 0.0, uh_ref[...])
    s = jnp.concatenate([halo, u], axis=0)
    for lvl in range(levels):
        s = s + pltpu.roll(s, 1 << lvl, 0)
    tpos = (blk * tm + lax.broadcasted_iota(jnp.int32, (tm, 1), 0)).astype(F32)
    inv_cnt = 1.0 / jnp.minimum(tpos + 1.0, float(1 << levels))
    y = s[POOL_HALO:, :] * inv_cnt - u + b_ref[...] * sc_ref[...]
    for rows in _row_slabs(tm, EPILOGUE_ROWS):
        z = jnp.dot(a_ref[rows, :], w_ref[...], preferred_element_type=F32)
        o_ref[rows, :] = (y[rows.start:rows.start + rows.size, :] * _silu(z)).astype(o_ref.dtype)


def _pool_gate_kernel(*refs, tm, bps):
    g = pl.program_id(0)
    blk = pl.program_id(1) % bps
    for k, win in enumerate(POOL_WINDOWS):
        @pl.when(g == k)
        def _(levels=win.bit_length() - 1):
            _pool_gate_body(*refs, levels=levels, blk=blk, tm=tm)


def _proj_in(body, a, w, col0, ncols, out_dtype, *, extra=(), extra_specs=(),
             prefetch=None, casts=(), w_layer=0, rows_outer=False, tm, tn=1024, name):
    t, kdim = a.shape
    jb = col0 // tn
    nb, nj = t // tm, ncols // tn
    if rows_outer:
        grid = (nb, nj)
        order = lambda f: (lambda i, j, *r: f(j, i, *r))
        step = lambda j, i: i * nj + j
    else:
        grid = (nj, nb)
        order = lambda f: f
        step = lambda j, i: j * nb + i
    spec = lambda shape, f: pl.BlockSpec(shape, order(f))
    c_in, c_out, c_shape = _cast_specs(casts, nb * nj, order(lambda j, i, *_: step(j, i)))
    in_specs = [spec((tm, kdim), lambda j, i, *_: (i, 0)),
                spec((None, kdim, tn), lambda j, i, *_: (w_layer, 0, jb + j))]
    in_specs += [spec(shape, f) for shape, f in extra_specs] + c_in
    out_specs = [spec((tm, tn), lambda j, i, *_: (i, j))] + c_out
    out_shape = [jax.ShapeDtypeStruct((t, ncols), out_dtype)] + c_shape
    n_pre = 0 if prefetch is None else 1
    kern = _with_side_casts(body, n_pre, 2 + len(extra), len(casts))
    gs = pltpu.PrefetchScalarGridSpec(num_scalar_prefetch=n_pre, grid=grid,
                                      in_specs=in_specs, out_specs=out_specs)
    args = ([] if prefetch is None else [prefetch]) + [a, w, *extra] + [c[0] for c in casts]
    outs = pl.pallas_call(kern, out_shape=out_shape, grid_spec=gs,
                          compiler_params=_params(("arbitrary", "arbitrary")), name=name)(*args)
    return outs[0], list(outs[1:])


def _proj_out_kernel(a_ref, w_ref, r_ref, g_ref, *o_refs, last):
    h = r_ref[...] + jnp.dot(a_ref[...], w_ref[...], preferred_element_type=F32)
    ms = jnp.mean(h * h, axis=-1, keepdims=True)
    hn = h * lax.rsqrt(ms + EPS) * g_ref[...]
    if last:
        o_refs[0][...] = hn
    else:
        o_refs[0][...] = h
        o_refs[1][...] = hn.astype(BF16)


def _proj_out(a, w, resid, g, g_layer, *, last, tm=512):
    t, kdim = a.shape
    n = w.shape[1]
    row = lambda i: (i, 0)
    in_specs = [pl.BlockSpec((tm, kdim), row),
                pl.BlockSpec((kdim, n), lambda i: (0, 0), pipeline_mode=pl.Buffered(1)),
                pl.BlockSpec((tm, n), row),
                pl.BlockSpec((None, 1, n), lambda i: (g_layer, 0, 0))]
    if last:
        out_shape = jax.ShapeDtypeStruct((t, n), F32)
        out_specs = pl.BlockSpec((tm, n), row)
    else:
        out_shape = (jax.ShapeDtypeStruct((t, n), F32), jax.ShapeDtypeStruct((t, n), BF16))
        out_specs = (pl.BlockSpec((tm, n), row), pl.BlockSpec((tm, n), row))
    return pl.pallas_call(
        functools.partial(_proj_out_kernel, last=last),
        out_shape=out_shape,
        grid=(t // tm,),
        in_specs=in_specs,
        out_specs=out_specs,
        compiler_params=_params(("arbitrary",)),
        name="proj_out",
    )(a, w, resid, g.reshape(g.shape[0], 1, n))


def _retention_kernel(lg_ref, q_ref, k_ref, v_ref, sz_ref, gn_ref, o_ref,
                      state_ref, p_ref, upd_ref, sb_ref, *, chunk, n_sub):
    h = pl.program_id(1)
    c = pl.program_id(2)

    @pl.when(c == 0)
    def _():
        state_ref[...] = jnp.zeros_like(state_ref)

    chunk_decay = jnp.exp(jnp.full((1, 1), float(chunk), F32) * lg_ref[h])
    ii = lax.broadcasted_iota(jnp.int32, (chunk, chunk), 0)
    jj = lax.broadcasted_iota(jnp.int32, (chunk, chunk), 1)
    causal = ii >= jj
    gn = gn_ref[...]

    for s in range(n_sub):
        rows = pl.ds(s * chunk, chunk)
        k = k_ref[rows, :]
        scores = lax.dot_general(q_ref[rows, :], k, (((1,), (1,)), ((), ())),
                                 preferred_element_type=F32)
        p_ref[s] = jnp.where(causal, scores, 0.0).astype(BF16)
        upd_ref[s] = lax.dot_general(k, v_ref[rows, :], (((0,), (0,)), ((), ())),
                                     preferred_element_type=F32)

    state = state_ref[...]
    for s in range(n_sub):
        sb_ref[s] = state.astype(BF16)
        state = chunk_decay * (state + upd_ref[s])
    state_ref[...] = state

    for s in range(n_sub):
        rows = pl.ds(s * chunk, chunk)
        lhs = jnp.concatenate([p_ref[s], q_ref[rows, :]], axis=1)
        rhs = jnp.concatenate([v_ref[rows, :], sb_ref[s]], axis=0)
        o = jnp.dot(lhs, rhs, preferred_element_type=F32)

        mu = jnp.mean(o, axis=-1, keepdims=True)
        oc = o - mu
        var = jnp.mean(oc * oc, axis=-1, keepdims=True)
        y = oc * lax.rsqrt(var + EPS) * gn
        o_ref[rows, :] = (y * sz_ref[rows, :].astype(F32)).astype(o_ref.dtype)


def _retention(p, lg, gn, layer, *, batch, seq, rows=4096):
    t = p.shape[0]
    nh = N_RET_HEADS
    rows = min(rows, seq)
    cps = seq // rows
    n_sub = rows // RET_CHUNK
    kern = functools.partial(_retention_kernel, chunk=RET_CHUNK, n_sub=n_sub)
    v_blk = 2 * nh * HEAD_QK // HEAD_V

    def blk(first):
        return lambda b, h, c: (b * cps + c, first + h)

    return pl.pallas_call(
        kern,
        out_shape=jax.ShapeDtypeStruct((t, nh * HEAD_V), BF16),
        grid=(batch, nh, cps),
        in_specs=[pl.BlockSpec(memory_space=pltpu.SMEM),
                  pl.BlockSpec((rows, HEAD_QK), blk(0)),
                  pl.BlockSpec((rows, HEAD_QK), blk(nh)),
                  pl.BlockSpec((rows, HEAD_V), blk(v_blk)),
                  pl.BlockSpec((rows, HEAD_V), blk(v_blk + nh)),
                  pl.BlockSpec((None, 1, HEAD_V), lambda b, h, c: (layer, 0, h))],
        out_specs=pl.BlockSpec((rows, HEAD_V), blk(0)),
        scratch_shapes=[pltpu.VMEM((HEAD_QK, HEAD_V), F32),
                        pltpu.VMEM((n_sub, RET_CHUNK, RET_CHUNK), BF16),
                        pltpu.VMEM((n_sub, HEAD_QK, HEAD_V), F32),
                        pltpu.VMEM((n_sub, HEAD_QK, HEAD_V), BF16)],
        compiler_params=_params(("arbitrary", "arbitrary", "arbitrary")),
        name="retention",
    )(lg, p, p, p, p, gn.reshape(gn.shape[0], 1, nh * HEAD_V))


def _rope_tables(seq):
    inv = ROPE_BASE ** (-jnp.arange(ROPE_HALF, dtype=F32) / ROPE_HALF)
    ang = jnp.arange(seq, dtype=F32)[:, None] * inv[None, :]
    return jnp.cos(ang), jnp.sin(ang)


def _retention_layer(hn, w_in, gn, layer, cos, sin, lg, casts, *, batch, seq, tm, tn=1024):
    qk_cols = 2 * N_RET_HEADS * HEAD_QK
    v_cols = N_RET_HEADS * HEAD_V
    bps = seq // tm
    kern = functools.partial(_ret_proj_kernel, col_axis=1, chunk=RET_CHUNK,
                             n_rope=qk_cols // tn, n_plain=v_cols // tn)
    tab = ((tm, ROPE_HALF), lambda j, i, *_: (i % bps, 0))
    p, got = _proj_in(kern, hn, w_in, 0, qk_cols + 2 * v_cols, BF16, extra=(cos, sin),
                      extra_specs=(tab, tab), prefetch=jnp.concatenate([lg, -lg]),
                      casts=casts, rows_outer=True, tm=tm, tn=tn, name="proj_ret")
    return _retention(p, lg, gn, layer, batch=batch, seq=seq), got


def _pooling_layer(hn, w_fold, fold_layer, w_z, b_grp, scale, layer, gate_casts, *, seq, tm):
    ng = len(POOL_WINDOWS)
    n_layers, width = b_grp.shape[0], w_z.shape[2]
    gdim = width // ng
    assert POOL_WINDOWS[-1] <= POOL_HALO
    u, _ = _proj_in(_plain_body, hn, w_fold, 0, width, F32, w_layer=fold_layer, rows_outer=True,
                    tm=tm, tn=gdim, name="proj_u")
    tg = tm // 2
    hb = tg // POOL_HALO
    row_vec = ((None, 1, gdim), lambda j, i: (layer, 0, j))
    extra_specs = (((tg, gdim), lambda j, i: (i, j)),
                   ((POOL_HALO, gdim), lambda j, i: (jnp.maximum(i * hb - 1, 0), j)),
                   row_vec, row_vec)
    return _proj_in(functools.partial(_pool_gate_kernel, tm=tg, bps=seq // tg),
                    hn, w_z, 0, width, BF16,
                    extra=(u, u, b_grp.reshape(n_layers, 1, width), scale.reshape(n_layers, 1, width)),
                    extra_specs=extra_specs, casts=gate_casts, tm=tg, tn=gdim, name="pool_gate")


def kernel(x, ret_norm, ret_w_in, ret_gn, ret_w_out, pool_norm, pool_w_in, pool_w_grp, pool_b_grp, pool_scale, pool_w_out, final_norm):
    batch, seq, d = x.shape
    n_ret, n_pool = ret_norm.shape[0], pool_norm.shape[0]
    depth = n_ret + n_pool
    tm = 2048
    width = pool_w_grp.shape[1] * pool_w_grp.shape[2]
    cos, sin = _rope_tables(seq)
    lg = jnp.log1p(-jnp.exp2(-5.0 - jnp.arange(N_RET_HEADS, dtype=F32)))

    h = x.reshape(batch * seq, d)
    hn, w_fold, (ret_in,) = _rmsnorm(h, ret_norm, 0, pool_w_in, pool_w_grp, pool_scale,
                                     [_whole(ret_w_in, 0)])
    pool = None
    for i in range(depth):
        j = i // 2
        if i % 2 == 0:
            has_pool = j < n_pool
            casts = [_whole(ret_w_out, j)]
            if has_pool:
                casts += [(pool_w_in, j, 1, width), _whole(pool_w_out, j)]
            act, got = _retention_layer(hn, ret_in[None], ret_gn, j, cos, sin, lg, casts,
                                        batch=batch, seq=seq, tm=tm)
            w_out = got[0]
            pool = tuple(got[1:]) if has_pool else None
        else:
            has_ret = j + 1 < n_ret
            act, got = _pooling_layer(hn, w_fold, j, pool[0][None], pool_b_grp, pool_scale, j,
                                      [_whole(ret_w_in, j + 1)] if has_ret else [], seq=seq, tm=tm)
            w_out = pool[1]
            ret_in = got[0] if has_ret else None
        if i + 1 == depth:
            out = _proj_out(act, w_out, h, final_norm.reshape(1, d), 0, last=True)
        else:
            nxt, nj = (pool_norm, j) if i % 2 == 0 else (ret_norm, j + 1)
            h, hn = _proj_out(act, w_out, h, nxt, nj, last=False)
    return out.reshape(batch, seq, d)
```

```python
import functools

import jax
import jax.numpy as jnp
from jax import lax
from jax.experimental import pallas as pl
from jax.experimental.pallas import tpu as pltpu

F32 = jnp.float32
BF16 = jnp.bfloat16

EPS = 1e-6
LOG2_E = 1.4426950408889634
N_RET_HEADS = 8
HEAD_QK = 256
HEAD_V = 512
ROPE_HALF = HEAD_QK // 2
ROPE_BASE = 10000.0
POOL_WINDOWS = (2, 4, 8, 16)
POOL_HALO = 16
RET_CHUNK = 256
MXU_ROWS = 1024
EPILOGUE_ROWS = 1024

VMEM_LIMIT = 56 * 1024 * 1024


def _params(sem):
    return pltpu.CompilerParams(dimension_semantics=sem, vmem_limit_bytes=VMEM_LIMIT)


def _silu(z):
    return z * (1.0 / (1.0 + jnp.exp2(z * (-LOG2_E))))


def _row_slabs(tm, slab):
    return [pl.ds(r, min(slab, tm)) for r in range(0, tm, slab)]


def _whole(src, layer):
    return (src, layer, 0, src.shape[2])


def _cast_specs(casts, steps, step_of):
    n = 1 << (steps.bit_length() - 1)
    in_specs, out_specs, out_shape = [], [], []
    slab = lambda *g: jnp.minimum(step_of(*g), n - 1)
    for src, layer, cb, width in casts:
        rb = src.shape[1] // n
        assert rb * n == src.shape[1] and rb % 16 == 0, (src.shape, n)
        in_specs.append(pl.BlockSpec((None, rb, width), lambda *g, l=layer, c=cb: (l, slab(*g), c)))
        out_specs.append(pl.BlockSpec((rb, width), lambda *g: (slab(*g), 0)))
        out_shape.append(jax.ShapeDtypeStruct((src.shape[1], width), BF16))
    return in_specs, out_specs, out_shape


def _with_side_casts(body, n_pre, n_in, n_cast, n_out=1):
    if n_cast == 0:
        return body

    def kern(*refs):
        head = n_pre + n_in
        outs = head + n_cast
        slabs_in = refs[head:outs]
        slabs_out = refs[outs + n_out:outs + n_out + n_cast]
        for src, dst in zip(slabs_in, slabs_out):
            dst[...] = src[...].astype(BF16)
        body(*refs[:head], *refs[outs:outs + n_out], *refs[outs + n_out + n_cast:])
    return kern


def _fold_slab(wu_ref, wg_ref, sc_ref, o_ref):
    prod = jnp.dot(wu_ref[...].astype(BF16), wg_ref[...].astype(BF16), preferred_element_type=F32)
    o_ref[...] = (prod * sc_ref[...]).astype(o_ref.dtype)


def _rmsnorm_kernel(x_ref, g_ref, wu_ref, wg_ref, sc_ref, o_ref, wf_ref):
    x = x_ref[...]
    ms = jnp.mean(x * x, axis=-1, keepdims=True)
    o_ref[...] = (x * lax.rsqrt(ms + EPS) * g_ref[...]).astype(o_ref.dtype)
    _fold_slab(wu_ref, wg_ref, sc_ref, wf_ref)


def _rmsnorm(x, g, layer, w_in, w_grp, scale, casts, tr=512):
    t, d = x.shape
    n_layers, ng, gdim, _ = w_grp.shape
    steps = t // tr
    n_slab = steps // (n_layers * ng)
    rs = d // n_slab
    assert n_slab * n_layers * ng == steps and rs * n_slab == d and rs % 16 == 0
    pos = lambda i: (i // (ng * n_slab), (i // n_slab) % ng, i % n_slab)
    c_in, c_out, c_shape = _cast_specs(casts, steps, lambda i: i)
    outs = pl.pallas_call(
        _with_side_casts(_rmsnorm_kernel, 0, 5, len(casts), n_out=2),
        out_shape=[jax.ShapeDtypeStruct((t, d), BF16),
                   jax.ShapeDtypeStruct((n_layers, d, ng * gdim), BF16)] + c_shape,
        grid=(steps,),
        in_specs=[pl.BlockSpec((tr, d), lambda i: (i, 0)),
                  pl.BlockSpec((None, 1, d), lambda i: (layer, 0, 0)),
                  pl.BlockSpec((None, rs, gdim), lambda i: (pos(i)[0], pos(i)[2], pos(i)[1])),
                  pl.BlockSpec((None, None, gdim, gdim), lambda i: (pos(i)[0], pos(i)[1], 0, 0)),
                  pl.BlockSpec((None, 1, gdim), lambda i: (pos(i)[0], 0, pos(i)[1]))] + c_in,
        out_specs=[pl.BlockSpec((tr, d), lambda i: (i, 0)),
                   pl.BlockSpec((None, rs, gdim), lambda i: (pos(i)[0], pos(i)[2], pos(i)[1]))] + c_out,
        compiler_params=_params(("arbitrary",)),
        name="rmsnorm",
    )(x, g.reshape(g.shape[0], 1, d), w_in, w_grp, scale.reshape(n_layers, 1, ng * gdim),
      *[c[0] for c in casts])
    return outs[0], outs[1], list(outs[2:])


def _plain_body(a_ref, w_ref, o_ref):
    for rows in _row_slabs(a_ref.shape[0], MXU_ROWS):
        acc = jnp.dot(a_ref[rows, :], w_ref[...], preferred_element_type=F32)
        o_ref[rows, :] = acc.astype(o_ref.dtype)


def _silu_body(a_ref, w_ref, o_ref):
    for rows in _row_slabs(a_ref.shape[0], EPILOGUE_ROWS):
        acc = jnp.dot(a_ref[rows, :], w_ref[...], preferred_element_type=F32)
        o_ref[rows, :] = _silu(acc).astype(o_ref.dtype)


def _rope_body(lg_ref, a_ref, w_ref, cos_ref, sin_ref, o_ref, *, j, chunk):
    heads_per_tile = o_ref.shape[1] // HEAD_QK
    for rows in _row_slabs(a_ref.shape[0], EPILOGUE_ROWS):
        acc = jnp.dot(a_ref[rows, :], w_ref[...], preferred_element_type=F32)
        cos = cos_ref[rows, :]
        sin = sin_ref[rows, :]
        row = rows.start + lax.broadcasted_iota(jnp.int32, (rows.size, 1), 0)
        pos1 = ((row % chunk) + 1).astype(F32)
        for hh in range(heads_per_tile):
            slot = j * heads_per_tile + hh
            fac = jnp.exp(pos1 * lg_ref[slot])
            fac = fac * jnp.where(slot < N_RET_HEADS, 1.0, HEAD_QK ** -0.5).astype(F32)
            cf = cos * fac
            sf = sin * fac
            a = hh * HEAD_QK
            x1 = acc[:, a:a + ROPE_HALF]
            x2 = acc[:, a + ROPE_HALF:a + HEAD_QK]
            o_ref[rows, a:a + ROPE_HALF] = (x1 * cf - x2 * sf).astype(o_ref.dtype)
            o_ref[rows, a + ROPE_HALF:a + HEAD_QK] = (x1 * sf + x2 * cf).astype(o_ref.dtype)


def _ret_proj_kernel(lg_ref, a_ref, w_ref, cos_ref, sin_ref, o_ref, *, col_axis, chunk, n_rope, n_plain):
    j = pl.program_id(col_axis)

    @pl.when(j < n_rope)
    def _():
        _rope_body(lg_ref, a_ref, w_ref, cos_ref, sin_ref, o_ref, j=j, chunk=chunk)

    @pl.when(jnp.logical_and(j >= n_rope, j < n_rope + n_plain))
    def _():
        _plain_body(a_ref, w_ref, o_ref)

    @pl.when(j >= n_rope + n_plain)
    def _():
        _silu_body(a_ref, w_ref, o_ref)


def _pool_gate_body(a_ref, w_ref, u_ref, uh_ref, b_ref, sc_ref, o_ref, *, levels, blk, tm):
    u = u_ref[...]
    halo = jnp.where(blk == 0, 0.0, uh_ref[...])
    s = jnp.concatenate([halo, u], axis=0)
    for lvl in range(levels):
        s = s + pltpu.roll(s, 1 << lvl, 0)
    tpos = (blk * tm + lax.broadcasted_iota(jnp.int32, (tm, 1), 0)).astype(F32)
    inv_cnt = 1.0 / jnp.minimum(tpos + 1.0, float(1 << levels))
    y = s[POOL_HALO:, :] * inv_cnt - u
    z = jnp.dot(a_ref[...], w_ref[...], preferred_element_type=F32)
    o_ref[...] = ((y + b_ref[...] * sc_ref[...]) * _silu(z)).astype(o_ref.dtype)


def _pool_gate_kernel(*refs, tm, bps):
    g = pl.program_id(0)
    blk = pl.program_id(1) % bps
    for k, win in enumerate(POOL_WINDOWS):
        @pl.when(g == k)
        def _(levels=win.bit_length() - 1):
            _pool_gate_body(*refs, levels=levels, blk=blk, tm=tm)


def _proj_in(body, a, w, col0, ncols, out_dtype, *, extra=(), extra_specs=(),
             prefetch=None, casts=(), w_layer=0, rows_outer=False, tm, tn=1024, name):
    t, kdim = a.shape
    jb = col0 // tn
    nb, nj = t // tm, ncols // tn
    if rows_outer:
        grid = (nb, nj)
        order = lambda f: (lambda i, j, *r: f(j, i, *r))
        step = lambda j, i: i * nj + j
    else:
        grid = (nj, nb)
        order = lambda f: f
        step = lambda j, i: j * nb + i
    spec = lambda shape, f: pl.BlockSpec(shape, order(f))
    c_in, c_out, c_shape = _cast_specs(casts, nb * nj, order(lambda j, i, *_: step(j, i)))
    in_specs = [spec((tm, kdim), lambda j, i, *_: (i, 0)),
                spec((None, kdim, tn), lambda j, i, *_: (w_layer, 0, jb + j))]
    in_specs += [spec(shape, f) for shape, f in extra_specs] + c_in
    out_specs = [spec((tm, tn), lambda j, i, *_: (i, j))] + c_out
    out_shape = [jax.ShapeDtypeStruct((t, ncols), out_dtype)] + c_shape
    n_pre = 0 if prefetch is None else 1
    kern = _with_side_casts(body, n_pre, 2 + len(extra), len(casts))
    gs = pltpu.PrefetchScalarGridSpec(num_scalar_prefetch=n_pre, grid=grid,
                                      in_specs=in_specs, out_specs=out_specs)
    args = ([] if prefetch is None else [prefetch]) + [a, w, *extra] + [c[0] for c in casts]
    outs = pl.pallas_call(kern, out_shape=out_shape, grid_spec=gs,
                          compiler_params=_params(("arbitrary", "arbitrary")), name=name)(*args)
    return outs[0], list(outs[1:])


def _proj_out_kernel(a_ref, w_ref, r_ref, g_ref, *o_refs, last):
    h = r_ref[...] + jnp.dot(a_ref[...], w_ref[...], preferred_element_type=F32)
    ms = jnp.mean(h * h, axis=-1, keepdims=True)
    hn = h * lax.rsqrt(ms + EPS) * g_ref[...]
    if last:
        o_refs[0][...] = hn
    else:
        o_refs[0][...] = h
        o_refs[1][...] = hn.astype(BF16)


def _proj_out(a, w, resid, g, g_layer, *, last, tm=512):
    t, kdim = a.shape
    n = w.shape[1]
    row = lambda i: (i, 0)
    in_specs = [pl.BlockSpec((tm, kdim), row),
                pl.BlockSpec((kdim, n), lambda i: (0, 0), pipeline_mode=pl.Buffered(1)),
                pl.BlockSpec((tm, n), row),
                pl.BlockSpec((None, 1, n), lambda i: (g_layer, 0, 0))]
    if last:
        out_shape = jax.ShapeDtypeStruct((t, n), F32)
        out_specs = pl.BlockSpec((tm, n), row)
    else:
        out_shape = (jax.ShapeDtypeStruct((t, n), F32), jax.ShapeDtypeStruct((t, n), BF16))
        out_specs = (pl.BlockSpec((tm, n), row), pl.BlockSpec((tm, n), row))
    return pl.pallas_call(
        functools.partial(_proj_out_kernel, last=last),
        out_shape=out_shape,
        grid=(t // tm,),
        in_specs=in_specs,
        out_specs=out_specs,
        compiler_params=_params(("arbitrary",)),
        name="proj_out",
    )(a, w, resid, g.reshape(g.shape[0], 1, n))


def _retention_kernel(lg_ref, q_ref, k_ref, v_ref, sz_ref, gn_ref, o_ref,
                      state_ref, p_ref, upd_ref, sb_ref, *, chunk, n_sub):
    h = pl.program_id(1)
    c = pl.program_id(2)

    @pl.when(c == 0)
    def _():
        state_ref[...] = jnp.zeros_like(state_ref)

    chunk_decay = jnp.exp(jnp.full((1, 1), float(chunk), F32) * lg_ref[h])
    ii = lax.broadcasted_iota(jnp.int32, (chunk, chunk), 0)
    jj = lax.broadcasted_iota(jnp.int32, (chunk, chunk), 1)
    causal = ii >= jj
    gn = gn_ref[...]

    for s in range(n_sub):
        rows = pl.ds(s * chunk, chunk)
        k = k_ref[rows, :]
        scores = lax.dot_general(q_ref[rows, :], k, (((1,), (1,)), ((), ())),
                                 preferred_element_type=F32)
        p_ref[s] = jnp.where(causal, scores, 0.0).astype(BF16)
        upd_ref[s] = lax.dot_general(k, v_ref[rows, :], (((0,), (0,)), ((), ())),
                                     preferred_element_type=F32)

    state = state_ref[...]
    for s in range(n_sub):
        sb_ref[s] = state.astype(BF16)
        state = chunk_decay * (state + upd_ref[s])
    state_ref[...] = state

    for s in range(n_sub):
        rows = pl.ds(s * chunk, chunk)
        lhs = jnp.concatenate([p_ref[s], q_ref[rows, :]], axis=1)
        rhs = jnp.concatenate([v_ref[rows, :], sb_ref[s]], axis=0)
        o = jnp.dot(lhs, rhs, preferred_element_type=F32)

        mu = jnp.mean(o, axis=-1, keepdims=True)
        oc = o - mu
        var = jnp.mean(oc * oc, axis=-1, keepdims=True)
        y = oc * lax.rsqrt(var + EPS) * gn
        o_ref[rows, :] = (y * sz_ref[rows, :].astype(F32)).astype(o_ref.dtype)


def _retention(p, lg, gn, layer, *, batch, seq, rows=4096):
    t = p.shape[0]
    nh = N_RET_HEADS
    rows = min(rows, seq)
    cps = seq // rows
    n_sub = rows // RET_CHUNK
    kern = functools.partial(_retention_kernel, chunk=RET_CHUNK, n_sub=n_sub)
    v_blk = 2 * nh * HEAD_QK // HEAD_V

    def blk(first):
        return lambda b, h, c: (b * cps + c, first + h)

    return pl.pallas_call(
        kern,
        out_shape=jax.ShapeDtypeStruct((t, nh * HEAD_V), BF16),
        grid=(batch, nh, cps),
        in_specs=[pl.BlockSpec(memory_space=pltpu.SMEM),
                  pl.BlockSpec((rows, HEAD_QK), blk(0)),
                  pl.BlockSpec((rows, HEAD_QK), blk(nh)),
                  pl.BlockSpec((rows, HEAD_V), blk(v_blk)),
                  pl.BlockSpec((rows, HEAD_V), blk(v_blk + nh)),
                  pl.BlockSpec((None, 1, HEAD_V), lambda b, h, c: (layer, 0, h))],
        out_specs=pl.BlockSpec((rows, HEAD_V), blk(0)),
        scratch_shapes=[pltpu.VMEM((HEAD_QK, HEAD_V), F32),
                        pltpu.VMEM((n_sub, RET_CHUNK, RET_CHUNK), BF16),
                        pltpu.VMEM((n_sub, HEAD_QK, HEAD_V), F32),
                        pltpu.VMEM((n_sub, HEAD_QK, HEAD_V), BF16)],
        compiler_params=_params(("arbitrary", "arbitrary", "arbitrary")),
        name="retention",
    )(lg, p, p, p, p, gn.reshape(gn.shape[0], 1, nh * HEAD_V))


def _rope_tables(seq):
    inv = ROPE_BASE ** (-jnp.arange(ROPE_HALF, dtype=F32) / ROPE_HALF)
    ang = jnp.arange(seq, dtype=F32)[:, None] * inv[None, :]
    return jnp.cos(ang), jnp.sin(ang)


def _retention_layer(hn, w_in, gn, layer, cos, sin, lg, casts, *, batch, seq, tm, tn=1024):
    qk_cols = 2 * N_RET_HEADS * HEAD_QK
    v_cols = N_RET_HEADS * HEAD_V
    bps = seq // tm
    kern = functools.partial(_ret_proj_kernel, col_axis=1, chunk=RET_CHUNK,
                             n_rope=qk_cols // tn, n_plain=v_cols // tn)
    tab = ((tm, ROPE_HALF), lambda j, i, *_: (i % bps, 0))
    p, got = _proj_in(kern, hn, w_in, 0, qk_cols + 2 * v_cols, BF16, extra=(cos, sin),
                      extra_specs=(tab, tab), prefetch=jnp.concatenate([lg, -lg]),
                      casts=casts, rows_outer=True, tm=tm, tn=tn, name="proj_ret")
    return _retention(p, lg, gn, layer, batch=batch, seq=seq), got


def _pooling_layer(hn, w_fold, fold_layer, w_z, b_grp, scale, layer, gate_casts, *, seq, tm):
    ng = len(POOL_WINDOWS)
    n_layers, width = b_grp.shape[0], w_z.shape[2]
    gdim = width // ng
    assert POOL_WINDOWS[-1] <= POOL_HALO
    u, _ = _proj_in(_plain_body, hn, w_fold, 0, width, F32, w_layer=fold_layer, rows_outer=True,
                    tm=tm, tn=gdim, name="proj_u")
    tg = tm // 2
    hb = tg // POOL_HALO
    row_vec = ((None, 1, gdim), lambda j, i: (layer, 0, j))
    extra_specs = (((tg, gdim), lambda j, i: (i, j)),
                   ((POOL_HALO, gdim), lambda j, i: (jnp.maximum(i * hb - 1, 0), j)),
                   row_vec, row_vec)
    return _proj_in(functools.partial(_pool_gate_kernel, tm=tg, bps=seq // tg),
                    hn, w_z, 0, width, BF16,
                    extra=(u, u, b_grp.reshape(n_layers, 1, width), scale.reshape(n_layers, 1, width)),
                    extra_specs=extra_specs, casts=gate_casts, tm=tg, tn=gdim, name="pool_gate")


def kernel(x, ret_norm, ret_w_in, ret_gn, ret_w_out, pool_norm, pool_w_in, pool_w_grp, pool_b_grp, pool_scale, pool_w_out, final_norm):
    batch, seq, d = x.shape
    n_ret, n_pool = ret_norm.shape[0], pool_norm.shape[0]
    depth = n_ret + n_pool
    tm = 2048
    width = pool_w_grp.shape[1] * pool_w_grp.shape[2]
    cos, sin = _rope_tables(seq)
    lg = jnp.log1p(-jnp.exp2(-5.0 - jnp.arange(N_RET_HEADS, dtype=F32)))

    h = x.reshape(batch * seq, d)
    hn, w_fold, (ret_in,) = _rmsnorm(h, ret_norm, 0, pool_w_in, pool_w_grp, pool_scale,
                                     [_whole(ret_w_in, 0)])
    pool = None
    for i in range(depth):
        j = i // 2
        if i % 2 == 0:
            has_pool = j < n_pool
            casts = [_whole(ret_w_out, j)]
            if has_pool:
                casts += [(pool_w_in, j, 1, width), _whole(pool_w_out, j)]
            act, got = _retention_layer(hn, ret_in[None], ret_gn, j, cos, sin, lg, casts,
                                        batch=batch, seq=seq, tm=tm)
            w_out = got[0]
            pool = tuple(got[1:]) if has_pool else None
        else:
            has_ret = j + 1 < n_ret
            act, got = _pooling_layer(hn, w_fold, j, pool[0][None], pool_b_grp, pool_scale, j,
                                      [_whole(ret_w_in, j + 1)] if has_ret else [], seq=seq, tm=tm)
            w_out = pool[1]
            ret_in = got[0] if has_ret else None
        if i + 1 == depth:
            out = _proj_out(act, w_out, h, final_norm.reshape(1, d), 0, last=True)
        else:
            nxt, nj = (pool_norm, j) if i % 2 == 0 else (ret_norm, j + 1)
            h, hn = _proj_out(act, w_out, h, nxt, nj, last=False)
    return out.reshape(batch, seq, d)
```

```python
import functools

import jax
import jax.numpy as jnp
from jax import lax
from jax.experimental import pallas as pl
from jax.experimental.pallas import tpu as pltpu

F32 = jnp.float32
BF16 = jnp.bfloat16

EPS = 1e-6
LOG2_E = 1.4426950408889634
N_RET_HEADS = 8
HEAD_QK = 256
HEAD_V = 512
ROPE_HALF = HEAD_QK // 2
ROPE_BASE = 10000.0
POOL_WINDOWS = (2, 4, 8, 16)
POOL_HALO = 16
RET_CHUNK = 256
MXU_ROWS = 512
EPILOGUE_ROWS = 256

VMEM_LIMIT = 56 * 1024 * 1024


def _params(sem):
    return pltpu.CompilerParams(dimension_semantics=sem, vmem_limit_bytes=VMEM_LIMIT)


def _silu(z):
    return z * (1.0 / (1.0 + jnp.exp2(z * (-LOG2_E))))


def _row_slabs(tm, slab):
    return [pl.ds(r, min(slab, tm)) for r in range(0, tm, slab)]


def _whole(src, layer):
    return (src, layer, 0, src.shape[2])


def _cast_specs(casts, steps, step_of):
    n = 1 << (steps.bit_length() - 1)
    in_specs, out_specs, out_shape = [], [], []
    slab = lambda *g: jnp.minimum(step_of(*g), n - 1)
    for src, layer, cb, width in casts:
        rb = src.shape[1] // n
        assert rb * n == src.shape[1] and rb % 16 == 0, (src.shape, n)
        in_specs.append(pl.BlockSpec((None, rb, width), lambda *g, l=layer, c=cb: (l, slab(*g), c)))
        out_specs.append(pl.BlockSpec((rb, width), lambda *g: (slab(*g), 0)))
        out_shape.append(jax.ShapeDtypeStruct((src.shape[1], width), BF16))
    return in_specs, out_specs, out_shape


def _with_side_casts(body, n_pre, n_in, n_cast, n_out=1):
    if n_cast == 0:
        return body

    def kern(*refs):
        head = n_pre + n_in
        outs = head + n_cast
        slabs_in = refs[head:outs]
        slabs_out = refs[outs + n_out:outs + n_out + n_cast]
        for src, dst in zip(slabs_in, slabs_out):
            dst[...] = src[...].astype(BF16)
        body(*refs[:head], *refs[outs:outs + n_out], *refs[outs + n_out + n_cast:])
    return kern


def _fold_slab(wu_ref, wg_ref, sc_ref, o_ref):
    prod = jnp.dot(wu_ref[...].astype(BF16), wg_ref[...].astype(BF16), preferred_element_type=F32)
    o_ref[...] = (prod * sc_ref[...]).astype(o_ref.dtype)


def _rmsnorm_kernel(x_ref, g_ref, wu_ref, wg_ref, sc_ref, o_ref, wf_ref):
    x = x_ref[...]
    ms = jnp.mean(x * x, axis=-1, keepdims=True)
    o_ref[...] = (x * lax.rsqrt(ms + EPS) * g_ref[...]).astype(o_ref.dtype)
    _fold_slab(wu_ref, wg_ref, sc_ref, wf_ref)


def _rmsnorm(x, g, layer, w_in, w_grp, scale, casts, tr=512):
    t, d = x.shape
    n_layers, ng, gdim, _ = w_grp.shape
    steps = t // tr
    n_slab = steps // (n_layers * ng)
    rs = d // n_slab
    assert n_slab * n_layers * ng == steps and rs * n_slab == d and rs % 16 == 0
    pos = lambda i: (i // (ng * n_slab), (i // n_slab) % ng, i % n_slab)
    c_in, c_out, c_shape = _cast_specs(casts, steps, lambda i: i)
    outs = pl.pallas_call(
        _with_side_casts(_rmsnorm_kernel, 0, 5, len(casts), n_out=2),
        out_shape=[jax.ShapeDtypeStruct((t, d), BF16),
                   jax.ShapeDtypeStruct((n_layers, d, ng * gdim), BF16)] + c_shape,
        grid=(steps,),
        in_specs=[pl.BlockSpec((tr, d), lambda i: (i, 0)),
                  pl.BlockSpec((None, 1, d), lambda i: (layer, 0, 0)),
                  pl.BlockSpec((None, rs, gdim), lambda i: (pos(i)[0], pos(i)[2], pos(i)[1])),
                  pl.BlockSpec((None, None, gdim, gdim), lambda i: (pos(i)[0], pos(i)[1], 0, 0)),
                  pl.BlockSpec((None, 1, gdim), lambda i: (pos(i)[0], 0, pos(i)[1]))] + c_in,
        out_specs=[pl.BlockSpec((tr, d), lambda i: (i, 0)),
                   pl.BlockSpec((None, rs, gdim), lambda i: (pos(i)[0], pos(i)[2], pos(i)[1]))] + c_out,
        compiler_params=_params(("arbitrary",)),
        name="rmsnorm",
    )(x, g.reshape(g.shape[0], 1, d), w_in, w_grp, scale.reshape(n_layers, 1, ng * gdim),
      *[c[0] for c in casts])
    return outs[0], outs[1], list(outs[2:])


def _plain_body(a_ref, w_ref, o_ref):
    for rows in _row_slabs(a_ref.shape[0], MXU_ROWS):
        acc = jnp.dot(a_ref[rows, :], w_ref[...], preferred_element_type=F32)
        o_ref[rows, :] = acc.astype(o_ref.dtype)


def _silu_body(a_ref, w_ref, o_ref):
    for rows in _row_slabs(a_ref.shape[0], EPILOGUE_ROWS):
        acc = jnp.dot(a_ref[rows, :], w_ref[...], preferred_element_type=F32)
        o_ref[rows, :] = _silu(acc).astype(o_ref.dtype)


def _rope_body(lg_ref, a_ref, w_ref, cos_ref, sin_ref, o_ref, *, j, chunk):
    heads_per_tile = o_ref.shape[1] // HEAD_QK
    for rows in _row_slabs(a_ref.shape[0], EPILOGUE_ROWS):
        acc = jnp.dot(a_ref[rows, :], w_ref[...], preferred_element_type=F32)
        cos = cos_ref[rows, :]
        sin = sin_ref[rows, :]
        row = rows.start + lax.broadcasted_iota(jnp.int32, (rows.size, 1), 0)
        pos1 = ((row % chunk) + 1).astype(F32)
        for hh in range(heads_per_tile):
            slot = j * heads_per_tile + hh
            fac = jnp.exp(pos1 * lg_ref[slot])
            fac = fac * jnp.where(slot < N_RET_HEADS, 1.0, HEAD_QK ** -0.5).astype(F32)
            cf = cos * fac
            sf = sin * fac
            a = hh * HEAD_QK
            x1 = acc[:, a:a + ROPE_HALF]
            x2 = acc[:, a + ROPE_HALF:a + HEAD_QK]
            o_ref[rows, a:a + ROPE_HALF] = (x1 * cf - x2 * sf).astype(o_ref.dtype)
            o_ref[rows, a + ROPE_HALF:a + HEAD_QK] = (x1 * sf + x2 * cf).astype(o_ref.dtype)


def _ret_proj_kernel(lg_ref, a_ref, w_ref, cos_ref, sin_ref, o_ref, *, col_axis, chunk, n_rope, n_plain):
    j = pl.program_id(col_axis)

    @pl.when(j < n_rope)
    def _():
        _rope_body(lg_ref, a_ref, w_ref, cos_ref, sin_ref, o_ref, j=j, chunk=chunk)

    @pl.when(jnp.logical_and(j >= n_rope, j < n_rope + n_plain))
    def _():
        _plain_body(a_ref, w_ref, o_ref)

    @pl.when(j >= n_rope + n_plain)
    def _():
        _silu_body(a_ref, w_ref, o_ref)


def _pool_gate_body(a_ref, wf_ref, w_ref, b_ref, sc_ref, o_ref, tail_ref, *, levels, blk, tm):
    u = jnp.dot(a_ref[...], wf_ref[...], preferred_element_type=F32)
    halo = jnp.where(blk == 0, 0.0, tail_ref[...])
    tail_ref[...] = u[tm - POOL_HALO:, :]
    s = jnp.concatenate([halo, u], axis=0)
    for lvl in range(levels):
        s = s + pltpu.roll(s, 1 << lvl, 0)
    tpos = (blk * tm + lax.broadcasted_iota(jnp.int32, (tm, 1), 0)).astype(F32)
    inv_cnt = 1.0 / jnp.minimum(tpos + 1.0, float(1 << levels))
    y = s[POOL_HALO:, :] * inv_cnt - u
    z = jnp.dot(a_ref[...], w_ref[...], preferred_element_type=F32)
    o_ref[...] = ((y + b_ref[...] * sc_ref[...]) * _silu(z)).astype(o_ref.dtype)


def _pool_gate_kernel(*refs, tm, bps):
    g = pl.program_id(0)
    blk = pl.program_id(1) % bps

    @pl.when(pl.program_id(1) == 0)
    def _():
        refs[-1][...] = jnp.zeros_like(refs[-1])

    for k, win in enumerate(POOL_WINDOWS):
        @pl.when(g == k)
        def _(levels=win.bit_length() - 1):
            _pool_gate_body(*refs, levels=levels, blk=blk, tm=tm)


def _proj_in(body, a, w, col0, ncols, out_dtype, *, extra=(), extra_specs=(),
             prefetch=None, casts=(), scratch=(), w_layer=0, rows_outer=False, tm, tn=1024, name):
    t, kdim = a.shape
    jb = col0 // tn
    nb, nj = t // tm, ncols // tn
    if rows_outer:
        grid = (nb, nj)
        order = lambda f: (lambda i, j, *r: f(j, i, *r))
        step = lambda j, i: i * nj + j
    else:
        grid = (nj, nb)
        order = lambda f: f
        step = lambda j, i: j * nb + i
    spec = lambda shape, f: pl.BlockSpec(shape, order(f))
    c_in, c_out, c_shape = _cast_specs(casts, nb * nj, order(lambda j, i, *_: step(j, i)))
    in_specs = [spec((tm, kdim), lambda j, i, *_: (i, 0)),
                spec((None, kdim, tn), lambda j, i, *_: (w_layer, 0, jb + j))]
    in_specs += [spec(shape, f) for shape, f in extra_specs] + c_in
    out_specs = [spec((tm, tn), lambda j, i, *_: (i, j))] + c_out
    out_shape = [jax.ShapeDtypeStruct((t, ncols), out_dtype)] + c_shape
    n_pre = 0 if prefetch is None else 1
    kern = _with_side_casts(body, n_pre, 2 + len(extra), len(casts))
    gs = pltpu.PrefetchScalarGridSpec(num_scalar_prefetch=n_pre, grid=grid,
                                      in_specs=in_specs, out_specs=out_specs,
                                      scratch_shapes=list(scratch))
    args = ([] if prefetch is None else [prefetch]) + [a, w, *extra] + [c[0] for c in casts]
    outs = pl.pallas_call(kern, out_shape=out_shape, grid_spec=gs,
                          compiler_params=_params(("arbitrary", "arbitrary")), name=name)(*args)
    return outs[0], list(outs[1:])


def _proj_out_kernel(a_ref, w_ref, r_ref, g_ref, *o_refs, last):
    h = r_ref[...] + jnp.dot(a_ref[...], w_ref[...], preferred_element_type=F32)
    ms = jnp.mean(h * h, axis=-1, keepdims=True)
    hn = h * lax.rsqrt(ms + EPS) * g_ref[...]
    if last:
        o_refs[0][...] = hn
    else:
        o_refs[0][...] = h
        o_refs[1][...] = hn.astype(BF16)


def _proj_out(a, w, resid, g, g_layer, *, last, tm=512):
    t, kdim = a.shape
    n = w.shape[1]
    row = lambda i: (i, 0)
    in_specs = [pl.BlockSpec((tm, kdim), row),
                pl.BlockSpec((kdim, n), lambda i: (0, 0), pipeline_mode=pl.Buffered(1)),
                pl.BlockSpec((tm, n), row),
                pl.BlockSpec((None, 1, n), lambda i: (g_layer, 0, 0))]
    if last:
        out_shape = jax.ShapeDtypeStruct((t, n), F32)
        out_specs = pl.BlockSpec((tm, n), row)
    else:
        out_shape = (jax.ShapeDtypeStruct((t, n), F32), jax.ShapeDtypeStruct((t, n), BF16))
        out_specs = (pl.BlockSpec((tm, n), row), pl.BlockSpec((tm, n), row))
    return pl.pallas_call(
        functools.partial(_proj_out_kernel, last=last),
        out_shape=out_shape,
        grid=(t // tm,),
        in_specs=in_specs,
        out_specs=out_specs,
        compiler_params=_params(("arbitrary",)),
        name="proj_out",
    )(a, w, resid, g.reshape(g.shape[0], 1, n))


def _retention_kernel(lg_ref, q_ref, k_ref, v_ref, sz_ref, gn_ref, o_ref,
                      state_ref, p_ref, upd_ref, sb_ref, *, chunk, n_sub):
    h = pl.program_id(1)
    c = pl.program_id(2)

    @pl.when(c == 0)
    def _():
        state_ref[...] = jnp.zeros_like(state_ref)

    chunk_decay = jnp.exp(jnp.full((1, 1), float(chunk), F32) * lg_ref[h])
    ii = lax.broadcasted_iota(jnp.int32, (chunk, chunk), 0)
    jj = lax.broadcasted_iota(jnp.int32, (chunk, chunk), 1)
    causal = ii >= jj
    gn = gn_ref[...]

    for s in range(n_sub):
        rows = pl.ds(s * chunk, chunk)
        k = k_ref[rows, :]
        scores = lax.dot_general(q_ref[rows, :], k, (((1,), (1,)), ((), ())),
                                 preferred_element_type=F32)
        p_ref[s] = jnp.where(causal, scores, 0.0).astype(BF16)
        upd_ref[s] = lax.dot_general(k, v_ref[rows, :], (((0,), (0,)), ((), ())),
                                     preferred_element_type=F32)

    state = state_ref[...]
    for s in range(n_sub):
        sb_ref[s] = state.astype(BF16)
        state = chunk_decay * (state + upd_ref[s])
    state_ref[...] = state

    for s in range(n_sub):
        rows = pl.ds(s * chunk, chunk)
        lhs = jnp.concatenate([p_ref[s], q_ref[rows, :]], axis=1)
        rhs = jnp.concatenate([v_ref[rows, :], sb_ref[s]], axis=0)
        o = jnp.dot(lhs, rhs, preferred_element_type=F32)

        mu = jnp.mean(o, axis=-1, keepdims=True)
        oc = o - mu
        var = jnp.mean(oc * oc, axis=-1, keepdims=True)
        y = oc * lax.rsqrt(var + EPS) * gn
        o_ref[rows, :] = (y * sz_ref[rows, :].astype(F32)).astype(o_ref.dtype)


def _retention(p, lg, gn, layer, *, batch, seq, rows=4096):
    t = p.shape[0]
    nh = N_RET_HEADS
    rows = min(rows, seq)
    cps = seq // rows
    n_sub = rows // RET_CHUNK
    kern = functools.partial(_retention_kernel, chunk=RET_CHUNK, n_sub=n_sub)
    v_blk = 2 * nh * HEAD_QK // HEAD_V

    def blk(first):
        return lambda b, h, c: (b * cps + c, first + h)

    return pl.pallas_call(
        kern,
        out_shape=jax.ShapeDtypeStruct((t, nh * HEAD_V), BF16),
        grid=(batch, nh, cps),
        in_specs=[pl.BlockSpec(memory_space=pltpu.SMEM),
                  pl.BlockSpec((rows, HEAD_QK), blk(0)),
                  pl.BlockSpec((rows, HEAD_QK), blk(nh)),
                  pl.BlockSpec((rows, HEAD_V), blk(v_blk)),
                  pl.BlockSpec((rows, HEAD_V), blk(v_blk + nh)),
                  pl.BlockSpec((None, 1, HEAD_V), lambda b, h, c: (layer, 0, h))],
        out_specs=pl.BlockSpec((rows, HEAD_V), blk(0)),
        scratch_shapes=[pltpu.VMEM((HEAD_QK, HEAD_V), F32),
                        pltpu.VMEM((n_sub, RET_CHUNK, RET_CHUNK), BF16),
                        pltpu.VMEM((n_sub, HEAD_QK, HEAD_V), F32),
                        pltpu.VMEM((n_sub, HEAD_QK, HEAD_V), BF16)],
        compiler_params=_params(("arbitrary", "arbitrary", "arbitrary")),
        name="retention",
    )(lg, p, p, p, p, gn.reshape(gn.shape[0], 1, nh * HEAD_V))


def _rope_tables(seq):
    inv = ROPE_BASE ** (-jnp.arange(ROPE_HALF, dtype=F32) / ROPE_HALF)
    ang = jnp.arange(seq, dtype=F32)[:, None] * inv[None, :]
    return jnp.cos(ang), jnp.sin(ang)


def _retention_layer(hn, w_in, gn, layer, cos, sin, lg, casts, *, batch, seq, tm, tn=1024):
    qk_cols = 2 * N_RET_HEADS * HEAD_QK
    v_cols = N_RET_HEADS * HEAD_V
    bps = seq // tm
    kern = functools.partial(_ret_proj_kernel, col_axis=1, chunk=RET_CHUNK,
                             n_rope=qk_cols // tn, n_plain=v_cols // tn)
    tab = ((tm, ROPE_HALF), lambda j, i, *_: (i % bps, 0))
    p, got = _proj_in(kern, hn, w_in, 0, qk_cols + 2 * v_cols, BF16, extra=(cos, sin),
                      extra_specs=(tab, tab), prefetch=jnp.concatenate([lg, -lg]),
                      casts=casts, rows_outer=True, tm=tm, tn=tn, name="proj_ret")
    return _retention(p, lg, gn, layer, batch=batch, seq=seq), got


def _pooling_layer(hn, w_fold, fold_layer, w_z, b_grp, scale, layer, gate_casts, *, seq, tm):
    ng = len(POOL_WINDOWS)
    n_layers, width = b_grp.shape[0], w_z.shape[2]
    gdim = width // ng
    assert POOL_WINDOWS[-1] <= POOL_HALO
    tg = tm // 2
    row_vec = ((None, 1, gdim), lambda j, i: (layer, 0, j))
    extra_specs = (((None, w_z.shape[1], gdim), lambda j, i: (0, 0, j)), row_vec, row_vec)
    return _proj_in(functools.partial(_pool_gate_kernel, tm=tg, bps=seq // tg),
                    hn, w_fold, 0, width, BF16, w_layer=fold_layer,
                    extra=(w_z, b_grp.reshape(n_layers, 1, width), scale.reshape(n_layers, 1, width)),
                    extra_specs=extra_specs, casts=gate_casts,
                    scratch=(pltpu.VMEM((POOL_HALO, gdim), F32),), tm=tg, tn=gdim, name="pool_front")


def kernel(x, ret_norm, ret_w_in, ret_gn, ret_w_out, pool_norm, pool_w_in, pool_w_grp, pool_b_grp, pool_scale, pool_w_out, final_norm):
    batch, seq, d = x.shape
    n_ret, n_pool = ret_norm.shape[0], pool_norm.shape[0]
    depth = n_ret + n_pool
    tm = 2048
    width = pool_w_grp.shape[1] * pool_w_grp.shape[2]
    cos, sin = _rope_tables(seq)
    lg = jnp.log1p(-jnp.exp2(-5.0 - jnp.arange(N_RET_HEADS, dtype=F32)))

    h = x.reshape(batch * seq, d)
    hn, w_fold, (ret_in,) = _rmsnorm(h, ret_norm, 0, pool_w_in, pool_w_grp, pool_scale,
                                     [_whole(ret_w_in, 0)])
    pool = None
    for i in range(depth):
        j = i // 2
        if i % 2 == 0:
            has_pool = j < n_pool
            casts = [_whole(ret_w_out, j)]
            if has_pool:
                casts += [(pool_w_in, j, 1, width), _whole(pool_w_out, j)]
            act, got = _retention_layer(hn, ret_in[None], ret_gn, j, cos, sin, lg, casts,
                                        batch=batch, seq=seq, tm=tm)
            w_out = got[0]
            pool = tuple(got[1:]) if has_pool else None
        else:
            has_ret = j + 1 < n_ret
            act, got = _pooling_layer(hn, w_fold, j, pool[0][None], pool_b_grp, pool_scale, j,
                                      [_whole(ret_w_in, j + 1)] if has_ret else [], seq=seq, tm=tm)
            w_out = pool[1]
            ret_in = got[0] if has_ret else None
        if i + 1 == depth:
            out = _proj_out(act, w_out, h, final_norm.reshape(1, d), 0, last=True)
        else:
            nxt, nj = (pool_norm, j) if i % 2 == 0 else (ret_norm, j + 1)
            h, hn = _proj_out(act, w_out, h, nxt, nj, last=False)
    return out.reshape(batch, seq, d)
```
